```python
import math
import jax, jax.numpy as jnp
from jax import lax
import numpy as np

D_MODEL = 1024
BATCH = 32
SEQ = 2048
DEPTH = 2

N_HEADS = 8
Q_LORA = 256
KV_LORA = 128
QK_NOPE = 64
QK_ROPE = 32
V_HEAD = 64
ROPE_THETA = 10000.0
Q_BLOCK = 128
D_HY = 512
SHORT_CONV = 3
FILTER_EMB = 33
FILTER_HIDDEN = 64
FAST_DECAY_PCT = 0.3
SLOW_DECAY_PCT = 1.5
DECAY_TARGET = 1e-2
MOD_SHIFT = 0.0
D_FF = -(-8 * D_MODEL // (3 * 256)) * 256
EPS = 1e-6
IN_SPLITS = (Q_LORA, Q_LORA + KV_LORA, Q_LORA + KV_LORA + QK_ROPE, Q_LORA + KV_LORA + QK_ROPE + 3 * D_HY)
IN_COLS = IN_SPLITS[-1] + 2 * D_MODEL

kernel_name = 'hybrid_mla_hyena_gated_encoder'


def rms_norm(x, g):
    xf = x.astype(jnp.float32)
    y = xf * lax.rsqrt(jnp.mean(xf * xf, axis=-1, keepdims=True) + EPS)
    return (y * g.astype(jnp.float32)).astype(x.dtype)


def rope_tables(positions):
    inv = 1.0 / (ROPE_THETA ** (jnp.arange(0, QK_ROPE, 2, dtype=jnp.float32) / QK_ROPE))
    ang = positions.astype(jnp.float32)[:, None] * inv[None, :]
    return jnp.cos(ang), jnp.sin(ang)


def apply_rope(x, cos, sin):
    xf = x.astype(jnp.float32).reshape(x.shape[:-1] + (QK_ROPE // 2, 2))
    bshape = (1, cos.shape[0]) + (1,) * (x.ndim - 3) + (cos.shape[1],)
    c = cos.reshape(bshape)
    s = sin.reshape(bshape)
    x0, x1 = xf[..., 0], xf[..., 1]
    return jnp.stack([x0 * c - x1 * s, x0 * s + x1 * c], axis=-1).reshape(x.shape).astype(x.dtype)


def mla_branch(q_lat, kv_lat, k_pe, q_norm_g, w_q_up, kv_norm_g, w_kv_up, w_attn_proj, cos, sin):
    b, s, _ = q_lat.shape
    q = (rms_norm(q_lat, q_norm_g) @ w_q_up).reshape(b, s, N_HEADS, QK_NOPE + QK_ROPE)
    q_nope = q[..., :QK_NOPE]
    q_pe = apply_rope(q[..., QK_NOPE:], cos, sin)
    kv = (rms_norm(kv_lat, kv_norm_g) @ w_kv_up).reshape(b, s, N_HEADS, QK_NOPE + V_HEAD)
    k_nope, v = kv[..., :QK_NOPE], kv[..., QK_NOPE:]
    k_pe = apply_rope(k_pe, cos, sin)
    scale = (QK_NOPE + QK_ROPE) ** -0.5
    nb = s // Q_BLOCK
    qn_blocks = jnp.moveaxis(q_nope.reshape(b, nb, Q_BLOCK, N_HEADS, QK_NOPE), 1, 0)
    qp_blocks = jnp.moveaxis(q_pe.reshape(b, nb, Q_BLOCK, N_HEADS, QK_ROPE), 1, 0)

    def attend(blk):
        qn, qp = blk
        logits = (jnp.einsum('bqhd,bkhd->bhqk', qn, k_nope, preferred_element_type=jnp.float32)
                  + jnp.einsum('bqhr,bkr->bhqk', qp, k_pe, preferred_element_type=jnp.float32)) * scale
        probs = jax.nn.softmax(logits, axis=-1).astype(v.dtype)
        return jnp.einsum('bhqk,bkhd->bqhd', probs, v)

    o = lax.map(attend, (qn_blocks, qp_blocks))
    o = jnp.moveaxis(o, 0, 1).reshape(b, s, N_HEADS * V_HEAD)
    return o @ w_attn_proj


def hyena_filters(length, w1, b1, f1, w2, b2, f2, w3, b3, f3, w4):
    f32 = jnp.float32
    t = jnp.linspace(0.0, 1.0, length, dtype=f32)[:, None]
    bands = (FILTER_EMB - 1) // 2
    freqs = jnp.linspace(1e-4, bands - 1, bands, dtype=f32)[None, :]
    w = 2.0 * math.pi * jnp.arange(length, dtype=f32)[:, None] / length
    z = jnp.concatenate([t, jnp.cos(freqs * w), -jnp.sin(freqs * w)], axis=-1)
    h = jnp.sin(f1.astype(f32) * (z @ w1.astype(f32) + b1.astype(f32)))
    h = jnp.sin(f2.astype(f32) * (h @ w2.astype(f32) + b2.astype(f32)))
    h = jnp.sin(f3.astype(f32) * (h @ w3.astype(f32) + b3.astype(f32)))
    h = h @ w4.astype(f32)
    deltas = jnp.abs(jnp.linspace(math.log(DECAY_TARGET) / FAST_DECAY_PCT,
                                  math.log(DECAY_TARGET) / SLOW_DECAY_PCT, D_HY, dtype=f32))
    decay = jnp.exp(-t * jnp.tile(deltas, 2)[None, :])
    return h * (decay + MOD_SHIFT)


def hyena_branch(u, conv_w, conv_b, w1, b1, f1, w2, b2, f2, w3, b3, f3, w4, hy_skip, w_hy_proj):
    b, length, _ = u.shape
    up = jnp.pad(u, ((0, 0), (1, 1), (0, 0)))
    uc = up[:, :-2] * conv_w[0] + up[:, 1:-1] * conv_w[1] + up[:, 2:] * conv_w[2] + conv_b
    x0, x1, v = jnp.split(uc, 3, axis=-1)
    filt = hyena_filters(length, w1, b1, f1, w2, b2, f2, w3, b3, f3, w4)
    taps = jnp.concatenate([filt[:, :D_HY], jnp.zeros((1, D_HY), jnp.float32),
                            jnp.flip(filt[1:, D_HY:], axis=0)], axis=0)
    z = (v * x1).astype(jnp.float32)
    spec = jnp.fft.rfft(z, n=2 * length, axis=1) * jnp.fft.rfft(taps, axis=0)[None]
    y = jnp.fft.irfft(spec, n=2 * length, axis=1)[:, :length] + z * hy_skip.astype(jnp.float32)
    y = (y * x0.astype(jnp.float32)).astype(u.dtype)
    return y @ w_hy_proj


def setup_inputs(seed: int = 0) -> dict:
    key = jax.random.key(seed)
    ks = iter(jax.random.split(key, 40))

    def nrm(shape, scale):
        return jax.random.normal(next(ks), shape, jnp.float32) * scale

    def gain(shape):
        return 1.0 + nrm(shape, 0.02)

    L = DEPTH
    x = nrm((BATCH, SEQ, D_MODEL), 1.0)
    positions = jnp.arange(SEQ, dtype=jnp.int32)
    mix_norm_g = gain((L, D_MODEL))
    w_in = nrm((L, D_MODEL, IN_COLS), D_MODEL ** -0.5)
    q_norm_g = gain((L, Q_LORA))
    w_q_up = nrm((L, Q_LORA, N_HEADS * (QK_NOPE + QK_ROPE)), Q_LORA ** -0.5)
    kv_norm_g = gain((L, KV_LORA))
    w_kv_up = nrm((L, KV_LORA, N_HEADS * (QK_NOPE + V_HEAD)), KV_LORA ** -0.5)
    w_attn_proj = nrm((L, N_HEADS * V_HEAD, D_MODEL), (N_HEADS * V_HEAD) ** -0.5)
    hy_conv_w = nrm((L, SHORT_CONV, 3 * D_HY), SHORT_CONV ** -0.5)
    hy_conv_b = nrm((L, 3 * D_HY), 0.02)
    filt_w1 = nrm((L, FILTER_EMB, FILTER_HIDDEN), FILTER_EMB ** -0.5)
    filt_b1 = nrm((L, FILTER_HIDDEN), 0.02)
    filt_f1 = gain((L, FILTER_HIDDEN))
    filt_w2 = nrm((L, FILTER_HIDDEN, FILTER_HIDDEN), FILTER_HIDDEN ** -0.5)
    filt_b2 = nrm((L, FILTER_HIDDEN), 0.02)
    filt_f2 = gain((L, FILTER_HIDDEN))
    filt_w3 = nrm((L, FILTER_HIDDEN, FILTER_HIDDEN), FILTER_HIDDEN ** -0.5)
    filt_b3 = nrm((L, FILTER_HIDDEN), 0.02)
    filt_f3 = gain((L, FILTER_HIDDEN))
    filt_w4 = nrm((L, FILTER_HIDDEN, 2 * D_HY), 0.1 * FILTER_HIDDEN ** -0.5)
    hy_skip = nrm((L, D_HY), 0.1)
    w_hy_proj = nrm((L, D_HY, D_MODEL), D_HY ** -0.5)
    w_out = nrm((L, D_MODEL, D_MODEL), D_MODEL ** -0.5)
    ffn_norm_g = gain((L, D_MODEL))
    w_gate = nrm((L, D_MODEL, D_FF), D_MODEL ** -0.5)
    w_up = nrm((L, D_MODEL, D_FF), D_MODEL ** -0.5)
    w_down = nrm((L, D_FF, D_MODEL), D_FF ** -0.5)
    final_norm_g = gain((D_MODEL,))
    return {'x': x, 'positions': positions, 'mix_norm_g': mix_norm_g, 'w_in': w_in,
            'q_norm_g': q_norm_g, 'w_q_up': w_q_up, 'kv_norm_g': kv_norm_g, 'w_kv_up': w_kv_up,
            'w_attn_proj': w_attn_proj, 'hy_conv_w': hy_conv_w, 'hy_conv_b': hy_conv_b,
            'filt_w1': filt_w1, 'filt_b1': filt_b1, 'filt_f1': filt_f1,
            'filt_w2': filt_w2, 'filt_b2': filt_b2, 'filt_f2': filt_f2,
            'filt_w3': filt_w3, 'filt_b3': filt_b3, 'filt_f3': filt_f3, 'filt_w4': filt_w4,
            'hy_skip': hy_skip, 'w_hy_proj': w_hy_proj, 'w_out': w_out,
            'ffn_norm_g': ffn_norm_g, 'w_gate': w_gate, 'w_up': w_up, 'w_down': w_down,
            'final_norm_g': final_norm_g}


def reference(x, positions, mix_norm_g, w_in, q_norm_g, w_q_up, kv_norm_g, w_kv_up,
              w_attn_proj, hy_conv_w, hy_conv_b, filt_w1, filt_b1, filt_f1,
              filt_w2, filt_b2, filt_f2, filt_w3, filt_b3, filt_f3, filt_w4,
              hy_skip, w_hy_proj, w_out, ffn_norm_g, w_gate, w_up, w_down, final_norm_g):
    cos, sin = rope_tables(positions)
    for i in range(DEPTH):
        h = rms_norm(x, mix_norm_g[i])
        proj = h @ w_in[i]
        q_lat, kv_lat, k_pe, hy_in, gate_logits = jnp.split(proj, IN_SPLITS, axis=-1)
        y_attn = mla_branch(q_lat, kv_lat, k_pe, q_norm_g[i], w_q_up[i], kv_norm_g[i], w_kv_up[i],
                            w_attn_proj[i], cos, sin)
        y_hy = hyena_branch(hy_in, hy_conv_w[i], hy_conv_b[i], filt_w1[i], filt_b1[i], filt_f1[i],
                            filt_w2[i], filt_b2[i], filt_f2[i], filt_w3[i], filt_b3[i], filt_f3[i],
                            filt_w4[i], hy_skip[i], w_hy_proj[i])
        gates = jax.nn.sigmoid(gate_logits.astype(jnp.float32))
        merged = (gates[..., :D_MODEL] * y_attn.astype(jnp.float32)
                  + gates[..., D_MODEL:] * y_hy.astype(jnp.float32)).astype(x.dtype)
        x = x + merged @ w_out[i]
        h = rms_norm(x, ffn_norm_g[i])
        x = x + (jax.nn.silu(h @ w_gate[i]) * (h @ w_up[i])) @ w_down[i]
    return rms_norm(x, final_norm_g)
```

```python
import functools
import math

import jax
import jax.numpy as jnp
from jax import lax
from jax.experimental import pallas as pl
from jax.experimental.pallas import tpu as pltpu

D_MODEL = 1024
N_HEADS = 8
Q_LORA = 256
KV_LORA = 128
QK_NOPE = 64
QK_ROPE = 32
V_HEAD = 64
ROPE_THETA = 10000.0
D_HY = 512
FILTER_EMB = 33
FILTER_HIDDEN = 64
FAST_DECAY_PCT = 0.3
SLOW_DECAY_PCT = 1.5
DECAY_TARGET = 1e-2
MOD_SHIFT = 0.0
EPS = 1e-6

LANES = 128
HEAD_PAD = LANES
QKV_W = N_HEADS * HEAD_PAD
P1_W = 512
VMEM_LIMIT = 56 * 1024 * 1024

TM_PROJ = 512
TM_MERGE = 512
TQ = 512
FCHUNK = 512
FFN_CHUNKS = ((0, 1536), (1536, 1280))

F32 = jnp.float32
BF16 = jnp.bfloat16


def _const_spec(shape):
    nd = len(shape)
    return pl.BlockSpec(shape, lambda *_: (0,) * nd, pipeline_mode=pl.Buffered(1))


def _rms(x, g):
    return x * lax.rsqrt(jnp.mean(x * x, axis=-1, keepdims=True) + EPS) * g


def _dot(a, b):
    return jnp.dot(a, b, preferred_element_type=F32)


def _rope_kernel(pos_ref, invq_ref, invk_ref, qtab_ref, kcos_ref, ksin_ref):
    pos = pos_ref[...].astype(F32)
    lane = lax.broadcasted_iota(jnp.int32, qtab_ref.shape, 1)
    scale = (QK_NOPE + QK_ROPE) ** -0.5
    angq = pos * invq_ref[...]
    rot = jnp.where(lane < QK_NOPE + QK_ROPE, jnp.cos(angq), jnp.sin(angq))
    qtab_ref[...] = jnp.where(lane < QK_NOPE, 1.0, rot) * scale
    angk = pos * invk_ref[...]
    kcos_ref[...] = jnp.where(lane < QK_ROPE, jnp.cos(angk), 0.0)
    ksin_ref[...] = jnp.where(lane < QK_ROPE, jnp.sin(angk), 0.0)


def _rope_tables(positions):
    s = positions.shape[0]
    inv = 1.0 / (ROPE_THETA ** (jnp.arange(0, QK_ROPE, 2, dtype=F32) / QK_ROPE))
    inv_pairs = jnp.repeat(inv, 2)
    zeros = jnp.zeros((QK_NOPE,), F32)
    invq = jnp.concatenate([zeros, inv_pairs, inv_pairs])[None, :]
    invk = jnp.concatenate([inv_pairs, jnp.zeros((LANES - QK_ROPE,), F32)])[None, :]
    out = jax.ShapeDtypeStruct((s, LANES), F32)
    return pl.pallas_call(
        _rope_kernel, out_shape=(out, out, out), name="rope_tables",
    )(positions.reshape(s, 1), invq, invk)


def _proj_kernel(x_ref, g_ref, w1_ref, why_ref, gq_ref, wq_ref, gkv_ref, wk_ref, wv_ref,
                 vone_ref, place_ref, qtab_ref, kcos_ref, ksin_ref,
                 q_out, k_out, v_out, hy_out):
    hb = _rms(x_ref[...], g_ref[...]).astype(BF16)
    p1 = _dot(hb, w1_ref[...])
    hy_out[...] = _dot(hb, why_ref[...]).astype(BF16)

    qn = _rms(p1[:, :Q_LORA], gq_ref[...]).astype(BF16)
    qtab = jnp.concatenate([qtab_ref[...]] * N_HEADS, axis=1)
    q_out[...] = (_dot(qn, wq_ref[...]) * qtab).astype(BF16)

    kvn = _rms(p1[:, Q_LORA:Q_LORA + KV_LORA], gkv_ref[...]).astype(BF16)
    t = p1[:, Q_LORA + KV_LORA:]
    k_rot = t * kcos_ref[...] + pltpu.roll(t, LANES - QK_ROPE, axis=1) * ksin_ref[...]
    k_out[...] = (_dot(kvn, wk_ref[...]) + _dot(k_rot.astype(BF16), place_ref[...])).astype(BF16)
    v_out[...] = (_dot(kvn, wv_ref[...]) + vone_ref[...]).astype(BF16)


def _proj_call(x2, seq, g, w1, why, gq, wq, gkv, wk, wv, vone, place, qtab, kcos, ksin):
    n = x2.shape[0]
    tm = TM_PROJ
    tiles_per_seq = seq // tm
    row = lambda i: (i, 0)
    tab = lambda i: (i % tiles_per_seq, 0)
    in_specs = [
        pl.BlockSpec((tm, D_MODEL), row),
        _const_spec(g.shape), _const_spec(w1.shape), _const_spec(why.shape),
        _const_spec(gq.shape), _const_spec(wq.shape), _const_spec(gkv.shape),
        _const_spec(wk.shape), _const_spec(wv.shape), _const_spec(vone.shape),
        _const_spec(place.shape),
        pl.BlockSpec((tm, LANES), tab), pl.BlockSpec((tm, LANES), tab),
        pl.BlockSpec((tm, LANES), tab),
    ]
    out_shape = (
        jax.ShapeDtypeStruct((n, QKV_W), BF16),
        jax.ShapeDtypeStruct((n, QKV_W), BF16),
        jax.ShapeDtypeStruct((n, QKV_W), BF16),
        jax.ShapeDtypeStruct((n, 3 * D_HY), BF16),
    )
    out_specs = (
        pl.BlockSpec((tm, QKV_W), row), pl.BlockSpec((tm, QKV_W), row),
        pl.BlockSpec((tm, QKV_W), row), pl.BlockSpec((tm, 3 * D_HY), row),
    )
    return pl.pallas_call(
        _proj_kernel, grid=(n // tm,), in_specs=in_specs, out_specs=out_specs,
        out_shape=out_shape, name="proj",
        compiler_params=pltpu.CompilerParams(
            dimension_semantics=("parallel",), vmem_limit_bytes=VMEM_LIMIT),
    )(x2, g, w1, why, gq, wq, gkv, wk, wv, vone, place, qtab, kcos, ksin)


def _attn_kernel(q_ref, k_ref, v_ref, o_ref):
    seq = q_ref.shape[1]
    lane = lax.broadcasted_iota(jnp.int32, (TQ, LANES), 1)

    def body(c, carry):
        r0 = pl.multiple_of(c * TQ, TQ)
        outs = []
        for hh in range(2):
            cols = slice(hh * HEAD_PAD, (hh + 1) * HEAD_PAD)
            q = q_ref[0, pl.ds(r0, TQ), cols]
            s = lax.dot_general(q, k_ref[0, :, cols], (((1,), (1,)), ((), ())),
                                preferred_element_type=F32)
            m = jnp.max(s, axis=1, keepdims=True)
            p = jnp.exp(s - m).astype(BF16)
            outs.append(_dot(p, v_ref[0, :, cols]))
        o0 = outs[0] / outs[0][:, V_HEAD:V_HEAD + 1]
        o1 = outs[1] / outs[1][:, 0:1]
        o_ref[0, pl.ds(r0, TQ), :] = jnp.where(lane < V_HEAD, o0, o1).astype(BF16)
        return carry

    lax.fori_loop(0, seq // TQ, body, 0)


def _attn_call(q3, k3, v3):
    b, seq, _ = q3.shape
    blk = pl.BlockSpec((1, seq, 2 * HEAD_PAD), lambda i, j: (i, 0, j))
    return pl.pallas_call(
        _attn_kernel, grid=(b, N_HEADS // 2), in_specs=[blk, blk, blk],
        out_specs=pl.BlockSpec((1, seq, 2 * V_HEAD), lambda i, j: (i, 0, j)),
        out_shape=jax.ShapeDtypeStruct((b, seq, N_HEADS * V_HEAD), BF16), name="attn",
        compiler_params=pltpu.CompilerParams(
            dimension_semantics=("parallel", "parallel"), vmem_limit_bytes=VMEM_LIMIT),
    )(q3, k3, v3)


def _hi_lo(a):
    hi = a.astype(BF16)
    return hi, (a - hi.astype(F32)).astype(BF16)


def _dot3(a, b):
    ah, al = _hi_lo(a)
    bh, bl = _hi_lo(b)
    return _dot(ah, bh) + (_dot(ah, bl) + _dot(al, bh))


def _filter_kernel(z_ref, decay_ref, w1, b1, f1, w2, b2, f2, w3, b3, f3, w4, u_ref, w_ref):
    h = jnp.sin(f1[0] * (_dot3(z_ref[...], w1[0]) + b1[0]))
    h = jnp.sin(f2[0] * (_dot3(h, w2[0]) + b2[0]))
    h = jnp.sin(f3[0] * (_dot3(h, w3[0]) + b3[0]))
    filt = _dot3(h, w4[0]) * (decay_ref[...] + MOD_SHIFT)
    fwd = filt[:, :D_HY]
    row = lax.broadcasted_iota(jnp.int32, fwd.shape, 0)
    bwd = jnp.where(row == 0, 0.0, filt[:, D_HY:])
    u_ref[0] = fwd + bwd
    w_ref[0] = bwd - fwd


def _spectrum_kernel(c_ref, s_ref, u_ref, w_ref, a_ref, b_ref, n_ref):
    n_fft = 2 * c_ref.shape[1]
    rows = c_ref.shape[0]
    row = lax.broadcasted_iota(jnp.int32, (rows, D_HY), 0) + pl.program_id(1) * rows
    first = row == 0
    kr = _dot3(c_ref[...], u_ref[0])
    ki = _dot3(s_ref[...], w_ref[0])
    a_ref[0] = kr * jnp.where(first, 1.0 / n_fft, 2.0 / n_fft)
    b_ref[0] = jnp.where(first, 0.0, ki * (2.0 / n_fft))
    n_ref[0] = _dot3(s_ref[0:8, :], u_ref[0]) * (1.0 / n_fft)


def _filter_spectrum(length, cos_t, sin_t, fw):
    depth = fw["w1"].shape[0]
    t = jnp.linspace(0.0, 1.0, length, dtype=F32)[:, None]
    bands = (FILTER_EMB - 1) // 2
    freqs = jnp.linspace(1e-4, bands - 1, bands, dtype=F32)[None, :]
    w = 2.0 * math.pi * jnp.arange(length, dtype=F32)[:, None] / length
    z = jnp.concatenate([t, jnp.cos(freqs * w), -jnp.sin(freqs * w)], axis=-1)
    deltas = jnp.abs(jnp.linspace(math.log(DECAY_TARGET) / FAST_DECAY_PCT,
                                  math.log(DECAY_TARGET) / SLOW_DECAY_PCT, D_HY, dtype=F32))
    decay = jnp.exp(-t * jnp.tile(deltas, 2)[None, :])

    z = jnp.pad(z, ((0, 0), (0, LANES - FILTER_EMB)))

    def lay(name):
        a = fw[name]
        if a.ndim == 2:
            a = a[:, None, :]
        rows = a.shape[1] if a.shape[1] == 1 else LANES
        cols = a.shape[2] if name == "w4" else LANES
        a = jnp.pad(a, ((0, 0), (0, rows - a.shape[1]), (0, cols - a.shape[2])))
        return a, pl.BlockSpec((1,) + a.shape[1:], lambda l: (l, 0, 0))

    names = ["w1", "b1", "f1", "w2", "b2", "f2", "w3", "b3", "f3", "w4"]
    arrs, specs = zip(*[lay(nm) for nm in names])
    uw = jax.ShapeDtypeStruct((depth, length, D_HY), F32)
    uw_spec = pl.BlockSpec((1, length, D_HY), lambda l: (l, 0, 0))
    u, wd = pl.pallas_call(
        _filter_kernel, grid=(depth,),
        in_specs=[_const_spec(z.shape), _const_spec(decay.shape), *specs],
        out_specs=(uw_spec, uw_spec), out_shape=(uw, uw), name="hyena_filter",
        compiler_params=pltpu.CompilerParams(vmem_limit_bytes=VMEM_LIMIT),
    )(z, decay, *arrs)

    rows = 256
    tab = pl.BlockSpec((rows, length), lambda l, i: (i, 0))
    full = pl.BlockSpec((1, length, D_HY), lambda l, i: (l, 0, 0))
    out = pl.BlockSpec((1, rows, D_HY), lambda l, i: (l, i, 0))
    return pl.pallas_call(
        _spectrum_kernel, grid=(depth, length // rows), in_specs=[tab, tab, full, full],
        out_specs=(out, out, pl.BlockSpec((1, 8, D_HY), lambda l, i: (l * (length // rows) + i, 0, 0))),
        out_shape=(uw, uw, jax.ShapeDtypeStruct((depth * (length // rows), 8, D_HY), F32)),
        name="hyena_spectrum",
        compiler_params=pltpu.CompilerParams(vmem_limit_bytes=VMEM_LIMIT),
    )(cos_t, sin_t, u, wd)


def _dft_tables(length):
    k = jnp.arange(length, dtype=jnp.int32)
    m = (k[:, None] * k[None, :]) % (2 * length)
    ang = m.astype(F32) * (math.pi / length)
    cos_t = jnp.cos(ang)
    nyq = jnp.where(k % 2 == 0, 1.0, -1.0).astype(F32)
    sin_t = jnp.where(k[:, None] == 0, nyq[None, :], jnp.sin(ang))
    return cos_t, sin_t


def _hyena_kernel(x0_ref, x1_ref, v_ref, cw_ref, cb_ref, skip_ref, a_ref, b_ref, n_ref,
                  cos_ref, sin_ref, sint_ref, y_ref, zf_s, zb_s, x0_s, pc_s, ps_s):
    length, ch = zf_s.shape
    row = lax.broadcasted_iota(jnp.int32, (length, ch), 0)

    def short_conv(ref, role):
        u = ref[0].astype(F32)
        prev = jnp.where(row == 0, 0.0, pltpu.roll(u, 1, axis=0))
        nxt = jnp.where(row == length - 1, 0.0, pltpu.roll(u, length - 1, axis=0))
        w0, w1, w2 = (cw_ref[3 * tap + role:3 * tap + role + 1, :] for tap in range(3))
        return prev * w0 + u * w1 + nxt * w2 + cb_ref[role:role + 1, :]

    x0_s[...] = short_conv(x0_ref, 0)
    z = short_conv(v_ref, 2) * short_conv(x1_ref, 1)
    zf_s[...] = z
    zb_s[...] = z.astype(BF16)

    for c in range(length // FCHUNK):
        rows = slice(c * FCHUNK, (c + 1) * FCHUNK)
        zc = _dot(cos_ref[rows, :], zb_s[...])
        zs = _dot(sin_ref[rows, :], zb_s[...])
        a = a_ref[0, rows, :]
        b = b_ref[0, rows, :]
        a2 = a
        if c == 0:
            first = lax.broadcasted_iota(jnp.int32, (FCHUNK, ch), 0) == 0
            a2 = jnp.where(first, n_ref[0, 0:1, :], a)
        pc_s[rows, :] = (zc * a + zs * b).astype(BF16)
        ps_s[rows, :] = (zs * a2 - zc * b).astype(BF16)

    for c in range(length // FCHUNK):
        rows = slice(c * FCHUNK, (c + 1) * FCHUNK)
        y = _dot(cos_ref[rows, :], pc_s[...]) + _dot(sint_ref[rows, :], ps_s[...])
        y = (y + zf_s[rows, :] * skip_ref[...]) * x0_s[rows, :]
        y_ref[0, rows, :] = y.astype(BF16)


def _hyena_call(hy3, cw, cb, skip, spec_a, spec_b, spec_n, layer, cos_b, sin_b, sint_b):
    b, length, _ = hy3.shape
    ch = D_HY // 2
    nh = D_HY // ch

    def role_spec(role):
        return pl.BlockSpec((1, length, ch), lambda j, i: (i, 0, role * nh + j))

    spec_blk = pl.BlockSpec((1, length, ch), lambda j, i: (layer, 0, j), pipeline_mode=pl.Buffered(1))
    nchunks = spec_n.shape[0] // spec_a.shape[0]
    in_specs = [
        role_spec(0), role_spec(1), role_spec(2),
        pl.BlockSpec((9, ch), lambda j, i: (0, j)),
        pl.BlockSpec((3, ch), lambda j, i: (0, j)),
        pl.BlockSpec((1, ch), lambda j, i: (0, j)),
        spec_blk, spec_blk,
        pl.BlockSpec((1, 8, ch), lambda j, i: (layer * nchunks, 0, j)),
        _const_spec(cos_b.shape), _const_spec(sin_b.shape), _const_spec(sint_b.shape),
    ]
    return pl.pallas_call(
        _hyena_kernel, grid=(nh, b), in_specs=in_specs,
        out_specs=pl.BlockSpec((1, length, ch), lambda j, i: (i, 0, j)),
        out_shape=jax.ShapeDtypeStruct((b, length, D_HY), BF16),
        scratch_shapes=[
            pltpu.VMEM((length, ch), F32), pltpu.VMEM((length, ch), BF16),
            pltpu.VMEM((length, ch), F32), pltpu.VMEM((length, ch), BF16),
            pltpu.VMEM((length, ch), BF16),
        ],
        name="hyena",
        compiler_params=pltpu.CompilerParams(
            dimension_semantics=("parallel", "parallel"), vmem_limit_bytes=VMEM_LIMIT),
    )(hy3, hy3, hy3, cw, cb, skip, spec_a, spec_b, spec_n, cos_b, sin_b, sint_b)


def _merge_kernel(final_norm, x_ref, ao_ref, hy_ref, gm_ref, wga_ref, wgb_ref, wap_ref, whp_ref,
                  wo_ref, gf_ref, wg_ref, wu_ref, wd_ref, gfin_ref, o_ref):
    x = x_ref[...]
    hb = _rms(x, gm_ref[...]).astype(BF16)
    merged = jax.nn.sigmoid(_dot(hb, wga_ref[...])) * _dot(ao_ref[...], wap_ref[...])
    merged = merged + jax.nn.sigmoid(_dot(hb, wgb_ref[...])) * _dot(hy_ref[...], whp_ref[...])
    x = x + _dot(merged.astype(BF16), wo_ref[...])
    hb = _rms(x, gf_ref[...]).astype(BF16)
    for start, width in FFN_CHUNKS:
        cols = slice(start, start + width)
        act = jax.nn.silu(_dot(hb, wg_ref[:, cols])) * _dot(hb, wu_ref[:, cols])
        x = x + _dot(act.astype(BF16), wd_ref[cols, :])
    if final_norm:
        x = _rms(x, gfin_ref[...])
    o_ref[...] = x


def _merge_call(x2, ao2, hy2, gm, wga, wgb, wap, whp, wo, gf, wg, wu, wd, gfin, final_norm):
    n = x2.shape[0]
    tm = TM_MERGE
    row = lambda i: (i, 0)
    consts = [gm, wga, wgb, wap, whp, wo, gf, wg, wu, wd, gfin]
    in_specs = [
        pl.BlockSpec((tm, D_MODEL), row),
        pl.BlockSpec((tm, ao2.shape[1]), row),
        pl.BlockSpec((tm, hy2.shape[1]), row),
    ] + [_const_spec(c.shape) for c in consts]
    return pl.pallas_call(
        functools.partial(_merge_kernel, final_norm), grid=(n // tm,), in_specs=in_specs,
        out_specs=pl.BlockSpec((tm, D_MODEL), row),
        out_shape=jax.ShapeDtypeStruct((n, D_MODEL), F32), name="merge_ffn",
        compiler_params=pltpu.CompilerParams(
            dimension_semantics=("parallel",), vmem_limit_bytes=VMEM_LIMIT),
    )(x2, ao2, hy2, *consts)


def _swap_pairs(w):
    w2 = w.reshape(w.shape[0], -1, 2)
    return jnp.stack([-w2[..., 1], w2[..., 0]], axis=-1).reshape(w.shape)


def _layer_weights(w_in, w_q_up, w_kv_up):
    c0, c1, c2, c3 = Q_LORA, Q_LORA + KV_LORA, Q_LORA + KV_LORA + QK_ROPE, Q_LORA + KV_LORA + QK_ROPE + 3 * D_HY
    w_kpe = w_in[:, c1:c2]
    pad = jnp.zeros((D_MODEL, P1_W - c2 - QK_ROPE), F32)
    w1 = jnp.concatenate([w_in[:, :c1], w_kpe, _swap_pairs(w_kpe), pad], axis=1).astype(BF16)
    why = w_in[:, c2:c3].astype(BF16)
    wga = w_in[:, c3:c3 + D_MODEL].astype(BF16)
    wgb = w_in[:, c3 + D_MODEL:].astype(BF16)

    wq3 = w_q_up.reshape(Q_LORA, N_HEADS, QK_NOPE + QK_ROPE)
    wq_pe = wq3[..., QK_NOPE:]
    wq = jnp.concatenate([wq3[..., :QK_NOPE], wq_pe, _swap_pairs(wq_pe.reshape(Q_LORA, -1)).reshape(wq_pe.shape)],
                         axis=-1).reshape(Q_LORA, QKV_W).astype(BF16)

    wkv3 = w_kv_up.reshape(KV_LORA, N_HEADS, QK_NOPE + V_HEAD)
    zpad = jnp.zeros((KV_LORA, N_HEADS, HEAD_PAD - QK_NOPE), F32)
    wk = jnp.concatenate([wkv3[..., :QK_NOPE], zpad], axis=-1).reshape(KV_LORA, QKV_W).astype(BF16)
    wv3 = wkv3[..., QK_NOPE:]
    odd = (jnp.arange(N_HEADS) % 2 == 1)[None, :, None]
    wv = jnp.where(odd, jnp.concatenate([zpad, wv3], axis=-1), jnp.concatenate([wv3, zpad], axis=-1))
    wv = wv.reshape(KV_LORA, QKV_W).astype(BF16)
    return w1, why, wga, wgb, wq, wk, wv


def _static_tables():
    lane = jnp.arange(QKV_W) % HEAD_PAD
    head = jnp.arange(QKV_W) // HEAD_PAD
    vone = jnp.where(head % 2 == 0, lane == V_HEAD, lane == 0).astype(F32)[None, :]
    src = jnp.arange(LANES)[:, None]
    place = ((src < QK_ROPE) & (lane[None, :] >= QK_NOPE) & ((lane[None, :] - QK_NOPE) % QK_ROPE == src))
    return vone, place.astype(BF16)


def kernel(x, positions, mix_norm_g, w_in, q_norm_g, w_q_up, kv_norm_g, w_kv_up, w_attn_proj, hy_conv_w, hy_conv_b, filt_w1, filt_b1, filt_f1, filt_w2, filt_b2, filt_f2, filt_w3, filt_b3, filt_f3, filt_w4, hy_skip, w_hy_proj, w_out, ffn_norm_g, w_gate, w_up, w_down, final_norm_g):
    b, seq, d = x.shape
    depth = w_in.shape[0]
    qtab, kcos, ksin = _rope_tables(positions)
    vone, place = _static_tables()
    cos_t, sin_t = _dft_tables(seq)
    cos_b, sin_b = cos_t.astype(BF16), sin_t.astype(BF16)
    sint_b = sin_b.T
    fw = dict(w1=filt_w1, b1=filt_b1, f1=filt_f1, w2=filt_w2, b2=filt_b2, f2=filt_f2,
              w3=filt_w3, b3=filt_b3, f3=filt_f3, w4=filt_w4)
    spec_a, spec_b, spec_n = _filter_spectrum(seq, cos_t, sin_t, fw)

    x2 = x.reshape(b * seq, d)
    for i in range(depth):
        w1, why, wga, wgb, wq, wk, wv = _layer_weights(w_in[i], w_q_up[i], w_kv_up[i])
        q2, k2, v2, hy2 = _proj_call(
            x2, seq, mix_norm_g[i][None], w1, why, q_norm_g[i][None], wq, kv_norm_g[i][None],
            wk, wv, vone, place, qtab, kcos, ksin)
        ao = _attn_call(q2.reshape(b, seq, QKV_W), k2.reshape(b, seq, QKV_W), v2.reshape(b, seq, QKV_W))
        hy = _hyena_call(
            hy2.reshape(b, seq, 3 * D_HY), hy_conv_w[i].reshape(9, D_HY), hy_conv_b[i].reshape(3, D_HY),
            hy_skip[i][None], spec_a, spec_b, spec_n, i, cos_b, sin_b, sint_b)
        x2 = _merge_call(
            x2, ao.reshape(b * seq, -1), hy.reshape(b * seq, -1), mix_norm_g[i][None], wga, wgb,
            w_attn_proj[i].astype(BF16), w_hy_proj[i].astype(BF16), w_out[i].astype(BF16),
            ffn_norm_g[i][None], w_gate[i].astype(BF16), w_up[i].astype(BF16), w_down[i].astype(BF16),
            final_norm_g[None], final_norm=(i == depth - 1))
    return x2.reshape(b, seq, d)
```

```python
import functools
import math

import jax
import jax.numpy as jnp
from jax import lax
from jax.experimental import pallas as pl
from jax.experimental.pallas import tpu as pltpu

D_MODEL = 1024
N_HEADS = 8
Q_LORA = 256
KV_LORA = 128
QK_NOPE = 64
QK_ROPE = 32
V_HEAD = 64
ROPE_THETA = 10000.0
D_HY = 512
FILTER_EMB = 33
FILTER_HIDDEN = 64
FAST_DECAY_PCT = 0.3
SLOW_DECAY_PCT = 1.5
DECAY_TARGET = 1e-2
MOD_SHIFT = 0.0
EPS = 1e-6

LANES = 128
HEAD_PAD = LANES
QKV_W = N_HEADS * HEAD_PAD
P1_W = 512
VMEM_LIMIT = 56 * 1024 * 1024

TM_PROJ = 512
TM_MERGE = 512
QT = 512
KC = 1024
FCHUNK = 512
FFN_CHUNKS = ((0, 1536), (1536, 1280))

F32 = jnp.float32
BF16 = jnp.bfloat16


def _const_spec(shape):
    nd = len(shape)
    return pl.BlockSpec(shape, lambda *_: (0,) * nd, pipeline_mode=pl.Buffered(1))


def _rms(x, g):
    return x * lax.rsqrt(jnp.mean(x * x, axis=-1, keepdims=True) + EPS) * g


def _dot(a, b):
    return jnp.dot(a, b, preferred_element_type=F32)


def _rope_kernel(pos_col_ref, pos_row_ref, invq_ref, invk_ref, qtab_ref, kcos_ref, ksin_ref):
    sub = lax.broadcasted_iota(jnp.int32, qtab_ref.shape, 0)
    scale = (QK_NOPE + QK_ROPE) ** -0.5 * math.log2(math.e)
    angq = pos_row_ref[...].astype(F32) * invq_ref[...]
    rot = jnp.where(sub < QK_NOPE + QK_ROPE, jnp.cos(angq), jnp.sin(angq))
    qtab_ref[...] = jnp.where(sub < QK_NOPE, 1.0, rot) * scale
    lane = lax.broadcasted_iota(jnp.int32, kcos_ref.shape, 1)
    angk = pos_col_ref[...].astype(F32) * invk_ref[...]
    kcos_ref[...] = jnp.where(lane < QK_ROPE, jnp.cos(angk), 0.0)
    ksin_ref[...] = jnp.where(lane < QK_ROPE, jnp.sin(angk), 0.0)


def _rope_tables(positions):
    s = positions.shape[0]
    inv = 1.0 / (ROPE_THETA ** (jnp.arange(0, QK_ROPE, 2, dtype=F32) / QK_ROPE))
    inv_pairs = jnp.repeat(inv, 2)
    zeros = jnp.zeros((QK_NOPE,), F32)
    invq = jnp.concatenate([zeros, inv_pairs, inv_pairs])[:, None]
    invk = jnp.concatenate([inv_pairs, jnp.zeros((LANES - QK_ROPE,), F32)])[None, :]
    return pl.pallas_call(
        _rope_kernel, name="rope_tables",
        out_shape=(jax.ShapeDtypeStruct((LANES, s), F32), jax.ShapeDtypeStruct((s, LANES), F32),
                   jax.ShapeDtypeStruct((s, LANES), F32)),
    )(positions.reshape(s, 1), positions.reshape(1, s), invq, invk)


_NT = (((1,), (1,)), ((), ()))


def _store_tiles(out_ref, val):
    width = out_ref.shape[2]
    for c in range(out_ref.shape[0]):
        out_ref[c] = val[:, c * width:(c + 1) * width]


def _tile_spec(tile, tm):
    if tile >= tm:
        per = tile // tm
        return pl.BlockSpec((1, QKV_W, tm), lambda i: (i // per, 0, i % per))
    return pl.BlockSpec((tm // tile, QKV_W, tile), lambda i: (i, 0, 0))


def _proj_kernel(x_ref, g_ref, w1_ref, why_ref, gq_ref, wqt_ref, gkv_ref, wk_ref, wvt_ref,
                 vone_ref, place_ref, qtab_ref, kcos_ref, ksin_ref,
                 q_out, k_out, v_out, hy_out):
    hb = _rms(x_ref[...], g_ref[...]).astype(BF16)
    p1 = _dot(hb, w1_ref[...])
    hy_out[...] = _dot(hb, why_ref[...]).astype(BF16)

    qn = _rms(p1[:, :Q_LORA], gq_ref[...]).astype(BF16)
    qtab = jnp.concatenate([qtab_ref[...]] * N_HEADS, axis=0)
    q_t = (lax.dot_general(wqt_ref[...], qn, _NT, preferred_element_type=F32) * qtab).astype(BF16)
    _store_tiles(q_out, q_t)

    kvn = _rms(p1[:, Q_LORA:Q_LORA + KV_LORA], gkv_ref[...]).astype(BF16)
    t = p1[:, Q_LORA + KV_LORA:]
    k_rot = t * kcos_ref[...] + pltpu.roll(t, LANES - QK_ROPE, axis=1) * ksin_ref[...]
    k_out[...] = (_dot(kvn, wk_ref[...]) + _dot(k_rot.astype(BF16), place_ref[...])).astype(BF16)
    v_t = lax.dot_general(wvt_ref[...], kvn, _NT, preferred_element_type=F32) + vone_ref[...]
    _store_tiles(v_out, v_t.astype(BF16))


def _proj_call(x2, seq, g, w1, why, gq, wqt, gkv, wk, wvt, vone, place, qtab, kcos, ksin):
    n = x2.shape[0]
    tm = TM_PROJ
    tiles_per_seq = seq // tm
    row = lambda i: (i, 0)
    tab = lambda i: (i % tiles_per_seq, 0)
    in_specs = [
        pl.BlockSpec((tm, D_MODEL), row),
        _const_spec(g.shape), _const_spec(w1.shape), _const_spec(why.shape),
        _const_spec(gq.shape), _const_spec(wqt.shape), _const_spec(gkv.shape),
        _const_spec(wk.shape), _const_spec(wvt.shape), _const_spec(vone.shape),
        _const_spec(place.shape),
        pl.BlockSpec((LANES, tm), lambda i: (0, i % tiles_per_seq)),
        pl.BlockSpec((tm, LANES), tab), pl.BlockSpec((tm, LANES), tab),
    ]
    out_shape = (
        jax.ShapeDtypeStruct((n // QT, QKV_W, QT), BF16),
        jax.ShapeDtypeStruct((n, QKV_W), BF16),
        jax.ShapeDtypeStruct((n // KC, QKV_W, KC), BF16),
        jax.ShapeDtypeStruct((n, 3 * D_HY), BF16),
    )
    out_specs = (
        _tile_spec(QT, tm), pl.BlockSpec((tm, QKV_W), row),
        _tile_spec(KC, tm), pl.BlockSpec((tm, 3 * D_HY), row),
    )
    return pl.pallas_call(
        _proj_kernel, grid=(n // tm,), in_specs=in_specs, out_specs=out_specs,
        out_shape=out_shape, name="proj",
        compiler_params=pltpu.CompilerParams(
            dimension_semantics=("parallel",), vmem_limit_bytes=VMEM_LIMIT),
    )(x2, g, w1, why, gq, wqt, gkv, wk, wvt, vone, place, qtab, kcos, ksin)


def _attn_kernel(q_ref, k_ref, v_ref, o_ref, s0_ref, s1_ref, acc_ref, o0_ref):
    n_tiles, n_chunks = q_ref.shape[0], v_ref.shape[0]
    s_bufs = (s0_ref, s1_ref)
    sub = lax.broadcasted_iota(jnp.int32, (HEAD_PAD, QT), 0)
    units = [(t, hh) for t in range(n_tiles) for hh in range(2)]

    def feat(hh):
        return slice(hh * HEAD_PAD, (hh + 1) * HEAD_PAD)

    def stage_a(kc, unit, buf, m_run):
        t, hh = unit
        keys = pl.ds(pl.multiple_of(kc * KC, KC), KC)
        sc = _dot(k_ref[0, keys, feat(hh)], q_ref[t, feat(hh), :])
        buf[keys, :] = sc
        return jnp.maximum(m_run, jnp.max(sc.reshape(KC // 8, 8, QT), axis=0))

    def stage_b(kc, unit, buf, m):
        _, hh = unit
        keys = pl.ds(pl.multiple_of(kc * KC, KC), KC)
        p = jnp.exp2((buf[keys, :] - m).astype(BF16))
        acc_ref[...] += _dot(v_ref[kc, feat(hh), :], p)

    neg = jnp.full((8, QT), -jnp.inf, F32)
    m_run = lax.fori_loop(0, n_chunks, lambda kc, mr: stage_a(kc, units[0], s_bufs[0], mr), neg)
    for u, unit in enumerate(units):
        m = jnp.max(m_run, axis=0, keepdims=True)
        acc_ref[...] = jnp.zeros_like(acc_ref)
        cur, nxt = s_bufs[u % 2], s_bufs[(u + 1) % 2]
        if u + 1 < len(units):
            def both(kc, mr, unit=unit, nxt_unit=units[u + 1], cur=cur, nxt=nxt, m=m):
                mr = stage_a(kc, nxt_unit, nxt, mr)
                stage_b(kc, unit, cur, m)
                return mr
            m_run = lax.fori_loop(0, n_chunks, both, neg)
        else:
            def last(kc, carry, unit=unit, cur=cur, m=m):
                stage_b(kc, unit, cur, m)
                return carry
            lax.fori_loop(0, n_chunks, last, 0)
        t, hh = unit
        acc = acc_ref[...]
        if hh == 0:
            o0_ref[...] = acc / acc[V_HEAD:V_HEAD + 1, :]
        else:
            o_t = jnp.where(sub < V_HEAD, o0_ref[...], acc / acc[0:1, :])
            o_ref[0, t * QT:(t + 1) * QT, :] = o_t.T.astype(BF16)


def _attn_call(q3, k3, vt):
    b, seq, _ = k3.shape
    return pl.pallas_call(
        _attn_kernel, grid=(b, N_HEADS // 2),
        in_specs=[
            pl.BlockSpec((seq // QT, 2 * HEAD_PAD, QT), lambda i, j: (i, j, 0)),
            pl.BlockSpec((1, seq, 2 * HEAD_PAD), lambda i, j: (i, 0, j)),
            pl.BlockSpec((seq // KC, 2 * HEAD_PAD, KC), lambda i, j: (i, j, 0)),
        ],
        out_specs=pl.BlockSpec((1, seq, 2 * V_HEAD), lambda i, j: (i, 0, j)),
        out_shape=jax.ShapeDtypeStruct((b, seq, N_HEADS * V_HEAD), BF16),
        scratch_shapes=[pltpu.VMEM((seq, QT), F32), pltpu.VMEM((seq, QT), F32),
                        pltpu.VMEM((HEAD_PAD, QT), F32), pltpu.VMEM((HEAD_PAD, QT), F32)],
        name="attn",
        compiler_params=pltpu.CompilerParams(
            dimension_semantics=("parallel", "parallel"), vmem_limit_bytes=VMEM_LIMIT),
    )(q3, k3, vt)


def _hi_lo(a):
    hi = a.astype(BF16)
    return hi, (a - hi.astype(F32)).astype(BF16)


def _dot3(a, b):
    ah, al = _hi_lo(a)
    bh, bl = _hi_lo(b)
    return _dot(ah, bh) + (_dot(ah, bl) + _dot(al, bh))


def _filter_kernel(z_ref, decay_ref, w1, b1, f1, w2, b2, f2, w3, b3, f3, w4, u_ref, w_ref):
    h = jnp.sin(f1[0] * (_dot3(z_ref[...], w1[0]) + b1[0]))
    h = jnp.sin(f2[0] * (_dot3(h, w2[0]) + b2[0]))
    h = jnp.sin(f3[0] * (_dot3(h, w3[0]) + b3[0]))
    filt = _dot3(h, w4[0]) * (decay_ref[...] + MOD_SHIFT)
    fwd = filt[:, :D_HY]
    row = lax.broadcasted_iota(jnp.int32, fwd.shape, 0)
    bwd = jnp.where(row == 0, 0.0, filt[:, D_HY:])
    u_ref[0] = fwd + bwd
    w_ref[0] = bwd - fwd


def _spectrum_kernel(c_ref, s_ref, u_ref, w_ref, a_ref, b_ref, n_ref):
    n_fft = 2 * c_ref.shape[1]
    rows = c_ref.shape[0]
    row = lax.broadcasted_iota(jnp.int32, (rows, D_HY), 0) + pl.program_id(1) * rows
    first = row == 0
    kr = _dot3(c_ref[...], u_ref[0])
    ki = _dot3(s_ref[...], w_ref[0])
    a_ref[0] = kr * jnp.where(first, 1.0 / n_fft, 2.0 / n_fft)
    b_ref[0] = jnp.where(first, 0.0, ki * (2.0 / n_fft))
    n_ref[0] = _dot3(s_ref[0:8, :], u_ref[0]) * (1.0 / n_fft)


def _filter_spectrum(length, cos_t, sin_t, fw):
    depth = fw["w1"].shape[0]
    t = jnp.linspace(0.0, 1.0, length, dtype=F32)[:, None]
    bands = (FILTER_EMB - 1) // 2
    freqs = jnp.linspace(1e-4, bands - 1, bands, dtype=F32)[None, :]
    w = 2.0 * math.pi * jnp.arange(length, dtype=F32)[:, None] / length
    z = jnp.concatenate([t, jnp.cos(freqs * w), -jnp.sin(freqs * w)], axis=-1)
    deltas = jnp.abs(jnp.linspace(math.log(DECAY_TARGET) / FAST_DECAY_PCT,
                                  math.log(DECAY_TARGET) / SLOW_DECAY_PCT, D_HY, dtype=F32))
    decay = jnp.exp(-t * jnp.tile(deltas, 2)[None, :])

    z = jnp.pad(z, ((0, 0), (0, LANES - FILTER_EMB)))

    def lay(name):
        a = fw[name]
        if a.ndim == 2:
            a = a[:, None, :]
        rows = a.shape[1] if a.shape[1] == 1 else LANES
        cols = a.shape[2] if name == "w4" else LANES
        a = jnp.pad(a, ((0, 0), (0, rows - a.shape[1]), (0, cols - a.shape[2])))
        return a, pl.BlockSpec((1,) + a.shape[1:], lambda l: (l, 0, 0))

    names = ["w1", "b1", "f1", "w2", "b2", "f2", "w3", "b3", "f3", "w4"]
    arrs, specs = zip(*[lay(nm) for nm in names])
    uw = jax.ShapeDtypeStruct((depth, length, D_HY), F32)
    uw_spec = pl.BlockSpec((1, length, D_HY), lambda l: (l, 0, 0))
    u, wd = pl.pallas_call(
        _filter_kernel, grid=(depth,),
        in_specs=[_const_spec(z.shape), _const_spec(decay.shape), *specs],
        out_specs=(uw_spec, uw_spec), out_shape=(uw, uw), name="hyena_filter",
        compiler_params=pltpu.CompilerParams(vmem_limit_bytes=VMEM_LIMIT),
    )(z, decay, *arrs)

    rows = 256
    tab = pl.BlockSpec((rows, length), lambda l, i: (i, 0))
    full = pl.BlockSpec((1, length, D_HY), lambda l, i: (l, 0, 0))
    out = pl.BlockSpec((1, rows, D_HY), lambda l, i: (l, i, 0))
    return pl.pallas_call(
        _spectrum_kernel, grid=(depth, length // rows), in_specs=[tab, tab, full, full],
        out_specs=(out, out, pl.BlockSpec((1, 8, D_HY), lambda l, i: (l * (length // rows) + i, 0, 0))),
        out_shape=(uw, uw, jax.ShapeDtypeStruct((depth * (length // rows), 8, D_HY), F32)),
        name="hyena_spectrum",
        compiler_params=pltpu.CompilerParams(vmem_limit_bytes=VMEM_LIMIT),
    )(cos_t, sin_t, u, wd)


def _dft_tables(length):
    k = jnp.arange(length, dtype=jnp.int32)
    m = (k[:, None] * k[None, :]) % (2 * length)
    ang = m.astype(F32) * (math.pi / length)
    cos_t = jnp.cos(ang)
    nyq = jnp.where(k % 2 == 0, 1.0, -1.0).astype(F32)
    sin_t = jnp.where(k[:, None] == 0, nyq[None, :], jnp.sin(ang))
    return cos_t, sin_t


def _hyena_kernel(x0_ref, x1_ref, v_ref, cw_ref, cb_ref, skip_ref, a_ref, b_ref, n_ref,
                  cos_ref, sin_ref, sint_ref, y_ref, zf_s, zb_s, x0_s, pc_s, ps_s):
    length, ch = zf_s.shape
    row = lax.broadcasted_iota(jnp.int32, (length, ch), 0)

    def short_conv(ref, role):
        u = ref[0].astype(F32)
        prev = jnp.where(row == 0, 0.0, pltpu.roll(u, 1, axis=0))
        nxt = jnp.where(row == length - 1, 0.0, pltpu.roll(u, length - 1, axis=0))
        w0, w1, w2 = (cw_ref[3 * tap + role:3 * tap + role + 1, :] for tap in range(3))
        return prev * w0 + u * w1 + nxt * w2 + cb_ref[role:role + 1, :]

    x0_s[...] = short_conv(x0_ref, 0)
    z = short_conv(v_ref, 2) * short_conv(x1_ref, 1)
    zf_s[...] = z
    zb_s[...] = z.astype(BF16)

    for c in range(length // FCHUNK):
        rows = slice(c * FCHUNK, (c + 1) * FCHUNK)
        zc = _dot(cos_ref[rows, :], zb_s[...])
        zs = _dot(sin_ref[rows, :], zb_s[...])
        a = a_ref[0, rows, :]
        b = b_ref[0, rows, :]
        a2 = a
        if c == 0:
            first = lax.broadcasted_iota(jnp.int32, (FCHUNK, ch), 0) == 0
            a2 = jnp.where(first, n_ref[0, 0:1, :], a)
        pc_s[rows, :] = (zc * a + zs * b).astype(BF16)
        ps_s[rows, :] = (zs * a2 - zc * b).astype(BF16)

    for c in range(length // FCHUNK):
        rows = slice(c * FCHUNK, (c + 1) * FCHUNK)
        y = _dot(cos_ref[rows, :], pc_s[...]) + _dot(sint_ref[rows, :], ps_s[...])
        y = (y + zf_s[rows, :] * skip_ref[...]) * x0_s[rows, :]
        y_ref[0, rows, :] = y.astype(BF16)


def _hyena_call(hy3, cw, cb, skip, spec_a, spec_b, spec_n, layer, cos_b, sin_b, sint_b):
    b, length, _ = hy3.shape
    ch = D_HY // 2
    nh = D_HY // ch

    def role_spec(role):
        return pl.BlockSpec((1, length, ch), lambda j, i: (i, 0, role * nh + j))

    spec_blk = pl.BlockSpec((1, length, ch), lambda j, i: (layer, 0, j), pipeline_mode=pl.Buffered(1))
    nchunks = spec_n.shape[0] // spec_a.shape[0]
    in_specs = [
        role_spec(0), role_spec(1), role_spec(2),
        pl.BlockSpec((9, ch), lambda j, i: (0, j)),
        pl.BlockSpec((3, ch), lambda j, i: (0, j)),
        pl.BlockSpec((1, ch), lambda j, i: (0, j)),
        spec_blk, spec_blk,
        pl.BlockSpec((1, 8, ch), lambda j, i: (layer * nchunks, 0, j)),
        _const_spec(cos_b.shape), _const_spec(sin_b.shape), _const_spec(sint_b.shape),
    ]
    return pl.pallas_call(
        _hyena_kernel, grid=(nh, b), in_specs=in_specs,
        out_specs=pl.BlockSpec((1, length, ch), lambda j, i: (i, 0, j)),
        out_shape=jax.ShapeDtypeStruct((b, length, D_HY), BF16),
        scratch_shapes=[
            pltpu.VMEM((length, ch), F32), pltpu.VMEM((length, ch), BF16),
            pltpu.VMEM((length, ch), F32), pltpu.VMEM((length, ch), BF16),
            pltpu.VMEM((length, ch), BF16),
        ],
        name="hyena",
        compiler_params=pltpu.CompilerParams(
            dimension_semantics=("parallel", "parallel"), vmem_limit_bytes=VMEM_LIMIT),
    )(hy3, hy3, hy3, cw, cb, skip, spec_a, spec_b, spec_n, cos_b, sin_b, sint_b)


def _merge_kernel(final_norm, x_ref, ao_ref, hy_ref, gm_ref, wga_ref, wgb_ref, wap_ref, whp_ref,
                  wo_ref, gf_ref, wg_ref, wu_ref, wd_ref, gfin_ref, o_ref):
    x = x_ref[...]
    hb = _rms(x, gm_ref[...]).astype(BF16)
    merged = jax.nn.sigmoid(_dot(hb, wga_ref[...])) * _dot(ao_ref[...], wap_ref[...])
    merged = merged + jax.nn.sigmoid(_dot(hb, wgb_ref[...])) * _dot(hy_ref[...], whp_ref[...])
    x = x + _dot(merged.astype(BF16), wo_ref[...])
    hb = _rms(x, gf_ref[...]).astype(BF16)
    for start, width in FFN_CHUNKS:
        cols = slice(start, start + width)
        act = jax.nn.silu(_dot(hb, wg_ref[:, cols])) * _dot(hb, wu_ref[:, cols])
        x = x + _dot(act.astype(BF16), wd_ref[cols, :])
    if final_norm:
        x = _rms(x, gfin_ref[...])
    o_ref[...] = x


def _merge_call(x2, ao2, hy2, gm, wga, wgb, wap, whp, wo, gf, wg, wu, wd, gfin, final_norm):
    n = x2.shape[0]
    tm = TM_MERGE
    row = lambda i: (i, 0)
    consts = [gm, wga, wgb, wap, whp, wo, gf, wg, wu, wd, gfin]
    in_specs = [
        pl.BlockSpec((tm, D_MODEL), row),
        pl.BlockSpec((tm, ao2.shape[1]), row),
        pl.BlockSpec((tm, hy2.shape[1]), row),
    ] + [_const_spec(c.shape) for c in consts]
    return pl.pallas_call(
        functools.partial(_merge_kernel, final_norm), grid=(n // tm,), in_specs=in_specs,
        out_specs=pl.BlockSpec((tm, D_MODEL), row),
        out_shape=jax.ShapeDtypeStruct((n, D_MODEL), F32), name="merge_ffn",
        compiler_params=pltpu.CompilerParams(
            dimension_semantics=("parallel",), vmem_limit_bytes=VMEM_LIMIT),
    )(x2, ao2, hy2, *consts)


def _swap_pairs(w):
    w2 = w.reshape(w.shape[0], -1, 2)
    return jnp.stack([-w2[..., 1], w2[..., 0]], axis=-1).reshape(w.shape)


def _layer_weights(w_in, w_q_up, w_kv_up):
    c0, c1, c2, c3 = Q_LORA, Q_LORA + KV_LORA, Q_LORA + KV_LORA + QK_ROPE, Q_LORA + KV_LORA + QK_ROPE + 3 * D_HY
    w_kpe = w_in[:, c1:c2]
    pad = jnp.zeros((D_MODEL, P1_W - c2 - QK_ROPE), F32)
    w1 = jnp.concatenate([w_in[:, :c1], w_kpe, _swap_pairs(w_kpe), pad], axis=1).astype(BF16)
    why = w_in[:, c2:c3].astype(BF16)
    wga = w_in[:, c3:c3 + D_MODEL].astype(BF16)
    wgb = w_in[:, c3 + D_MODEL:].astype(BF16)

    wq3 = w_q_up.reshape(Q_LORA, N_HEADS, QK_NOPE + QK_ROPE)
    wq_pe = wq3[..., QK_NOPE:]
    wq = jnp.concatenate([wq3[..., :QK_NOPE], wq_pe, _swap_pairs(wq_pe.reshape(Q_LORA, -1)).reshape(wq_pe.shape)],
                         axis=-1).reshape(Q_LORA, QKV_W).astype(BF16)

    wkv3 = w_kv_up.reshape(KV_LORA, N_HEADS, QK_NOPE + V_HEAD)
    zpad = jnp.zeros((KV_LORA, N_HEADS, HEAD_PAD - QK_NOPE), F32)
    wk = jnp.concatenate([wkv3[..., :QK_NOPE], zpad], axis=-1).reshape(KV_LORA, QKV_W).astype(BF16)
    wv3 = wkv3[..., QK_NOPE:]
    odd = (jnp.arange(N_HEADS) % 2 == 1)[None, :, None]
    wv = jnp.where(odd, jnp.concatenate([zpad, wv3], axis=-1), jnp.concatenate([wv3, zpad], axis=-1))
    wv = wv.reshape(KV_LORA, QKV_W).astype(BF16)
    return w1, why, wga, wgb, wq.T, wk, wv.T


def _static_tables():
    lane = jnp.arange(QKV_W) % HEAD_PAD
    head = jnp.arange(QKV_W) // HEAD_PAD
    vone = jnp.where(head % 2 == 0, lane == V_HEAD, lane == 0).astype(F32)[:, None]
    src = jnp.arange(LANES)[:, None]
    place = ((src < QK_ROPE) & (lane[None, :] >= QK_NOPE) & ((lane[None, :] - QK_NOPE) % QK_ROPE == src))
    return vone, place.astype(BF16)


def kernel(x, positions, mix_norm_g, w_in, q_norm_g, w_q_up, kv_norm_g, w_kv_up, w_attn_proj, hy_conv_w, hy_conv_b, filt_w1, filt_b1, filt_f1, filt_w2, filt_b2, filt_f2, filt_w3, filt_b3, filt_f3, filt_w4, hy_skip, w_hy_proj, w_out, ffn_norm_g, w_gate, w_up, w_down, final_norm_g):
    b, seq, d = x.shape
    depth = w_in.shape[0]
    qtab, kcos, ksin = _rope_tables(positions)
    vone, place = _static_tables()
    cos_t, sin_t = _dft_tables(seq)
    cos_b, sin_b = cos_t.astype(BF16), sin_t.astype(BF16)
    sint_b = sin_b.T
    fw = dict(w1=filt_w1, b1=filt_b1, f1=filt_f1, w2=filt_w2, b2=filt_b2, f2=filt_f2,
              w3=filt_w3, b3=filt_b3, f3=filt_f3, w4=filt_w4)
    spec_a, spec_b, spec_n = _filter_spectrum(seq, cos_t, sin_t, fw)

    x2 = x.reshape(b * seq, d)
    for i in range(depth):
        w1, why, wga, wgb, wqt, wk, wvt = _layer_weights(w_in[i], w_q_up[i], w_kv_up[i])
        q3, k2, vt, hy2 = _proj_call(
            x2, seq, mix_norm_g[i][None], w1, why, q_norm_g[i][None], wqt, kv_norm_g[i][None],
            wk, wvt, vone, place, qtab, kcos, ksin)
        ao = _attn_call(q3, k2.reshape(b, seq, QKV_W), vt)
        hy = _hyena_call(
            hy2.reshape(b, seq, 3 * D_HY), hy_conv_w[i].reshape(9, D_HY), hy_conv_b[i].reshape(3, D_HY),
            hy_skip[i][None], spec_a, spec_b, spec_n, i, cos_b, sin_b, sint_b)
        x2 = _merge_call(
            x2, ao.reshape(b * seq, -1), hy.reshape(b * seq, -1), mix_norm_g[i][None], wga, wgb,
            w_attn_proj[i].astype(BF16), w_hy_proj[i].astype(BF16), w_out[i].astype(BF16),
            ffn_norm_g[i][None], w_gate[i].astype(BF16), w_up[i].astype(BF16), w_down[i].astype(BF16),
            final_norm_g[None], final_norm=(i == depth - 1))
    return x2.reshape(b, seq, d)
```

```python
import functools
import math

import jax
import jax.numpy as jnp
from jax import lax
from jax.experimental import pallas as pl
from jax.experimental.pallas import tpu as pltpu

D_MODEL = 1024
N_HEADS = 8
Q_LORA = 256
KV_LORA = 128
QK_NOPE = 64
QK_ROPE = 32
V_HEAD = 64
ROPE_THETA = 10000.0
D_HY = 512
FILTER_EMB = 33
FILTER_HIDDEN = 64
FAST_DECAY_PCT = 0.3
SLOW_DECAY_PCT = 1.5
DECAY_TARGET = 1e-2
MOD_SHIFT = 0.0
EPS = 1e-6

LANES = 128
HEAD_PAD = LANES
QKV_W = N_HEADS * HEAD_PAD
P1_W = 512
VMEM_LIMIT = 56 * 1024 * 1024

TM_PROJ = 512
TM_MERGE = 512
QT = 512
KC = 1024
HC = 512
MIX_ROWS = 16
FFN_CHUNKS = ((0, 1536), (1536, 1280))

F32 = jnp.float32
BF16 = jnp.bfloat16


def _const_spec(shape):
    nd = len(shape)
    return pl.BlockSpec(shape, lambda *_: (0,) * nd, pipeline_mode=pl.Buffered(1))


def _rms(x, g):
    return x * lax.rsqrt(jnp.mean(x * x, axis=-1, keepdims=True) + EPS) * g


def _dot(a, b):
    return jnp.dot(a, b, preferred_element_type=F32)


def _rope_kernel(pos_col_ref, pos_row_ref, invq_ref, invk_ref, qtab_ref, kcos_ref, ksin_ref):
    sub = lax.broadcasted_iota(jnp.int32, qtab_ref.shape, 0)
    scale = (QK_NOPE + QK_ROPE) ** -0.5 * math.log2(math.e)
    angq = pos_row_ref[...].astype(F32) * invq_ref[...]
    rot = jnp.where(sub < QK_NOPE + QK_ROPE, jnp.cos(angq), jnp.sin(angq))
    qtab_ref[...] = jnp.where(sub < QK_NOPE, 1.0, rot) * scale
    lane = lax.broadcasted_iota(jnp.int32, kcos_ref.shape, 1)
    angk = pos_col_ref[...].astype(F32) * invk_ref[...]
    kcos_ref[...] = jnp.where(lane < QK_ROPE, jnp.cos(angk), 0.0)
    ksin_ref[...] = jnp.where(lane < QK_ROPE, jnp.sin(angk), 0.0)


def _rope_tables(positions):
    s = positions.shape[0]
    inv = 1.0 / (ROPE_THETA ** (jnp.arange(0, QK_ROPE, 2, dtype=F32) / QK_ROPE))
    inv_pairs = jnp.repeat(inv, 2)
    zeros = jnp.zeros((QK_NOPE,), F32)
    invq = jnp.concatenate([zeros, inv_pairs, inv_pairs])[:, None]
    invk = jnp.concatenate([inv_pairs, jnp.zeros((LANES - QK_ROPE,), F32)])[None, :]
    return pl.pallas_call(
        _rope_kernel, name="rope_tables",
        out_shape=(jax.ShapeDtypeStruct((LANES, s), F32), jax.ShapeDtypeStruct((s, LANES), F32),
                   jax.ShapeDtypeStruct((s, LANES), F32)),
    )(positions.reshape(s, 1), positions.reshape(1, s), invq, invk)


_NT = (((1,), (1,)), ((), ()))


def _store_tiles(out_ref, val):
    width = out_ref.shape[2]
    for c in range(out_ref.shape[0]):
        out_ref[c] = val[:, c * width:(c + 1) * width]


def _tile_spec(tile, tm):
    if tile >= tm:
        per = tile // tm
        return pl.BlockSpec((1, QKV_W, tm), lambda i: (i // per, 0, i % per))
    return pl.BlockSpec((tm // tile, QKV_W, tile), lambda i: (i, 0, 0))


def _proj_kernel(x_ref, g_ref, w1_ref, why_ref, gq_ref, wqt_ref, gkv_ref, wk_ref, wvt_ref,
                 vone_ref, place_ref, qtab_ref, kcos_ref, ksin_ref,
                 q_out, k_out, v_out, hy_out):
    hb = _rms(x_ref[...], g_ref[...]).astype(BF16)
    p1 = _dot(hb, w1_ref[...])
    hy_out[...] = _dot(hb, why_ref[...]).astype(BF16)

    qn = _rms(p1[:, :Q_LORA], gq_ref[...]).astype(BF16)
    qtab = jnp.concatenate([qtab_ref[...]] * N_HEADS, axis=0)
    q_t = (lax.dot_general(wqt_ref[...], qn, _NT, preferred_element_type=F32) * qtab).astype(BF16)
    _store_tiles(q_out, q_t)

    kvn = _rms(p1[:, Q_LORA:Q_LORA + KV_LORA], gkv_ref[...]).astype(BF16)
    t = p1[:, Q_LORA + KV_LORA:]
    k_rot = t * kcos_ref[...] + pltpu.roll(t, LANES - QK_ROPE, axis=1) * ksin_ref[...]
    k_out[...] = (_dot(kvn, wk_ref[...]) + _dot(k_rot.astype(BF16), place_ref[...])).astype(BF16)
    v_t = lax.dot_general(wvt_ref[...], kvn, _NT, preferred_element_type=F32) + vone_ref[...]
    _store_tiles(v_out, v_t.astype(BF16))


def _proj_call(x2, seq, g, w1, why, gq, wqt, gkv, wk, wvt, vone, place, qtab, kcos, ksin):
    n = x2.shape[0]
    tm = TM_PROJ
    tiles_per_seq = seq // tm
    row = lambda i: (i, 0)
    tab = lambda i: (i % tiles_per_seq, 0)
    in_specs = [
        pl.BlockSpec((tm, D_MODEL), row),
        _const_spec(g.shape), _const_spec(w1.shape), _const_spec(why.shape),
        _const_spec(gq.shape), _const_spec(wqt.shape), _const_spec(gkv.shape),
        _const_spec(wk.shape), _const_spec(wvt.shape), _const_spec(vone.shape),
        _const_spec(place.shape),
        pl.BlockSpec((LANES, tm), lambda i: (0, i % tiles_per_seq)),
        pl.BlockSpec((tm, LANES), tab), pl.BlockSpec((tm, LANES), tab),
    ]
    out_shape = (
        jax.ShapeDtypeStruct((n // QT, QKV_W, QT), BF16),
        jax.ShapeDtypeStruct((n, QKV_W), BF16),
        jax.ShapeDtypeStruct((n // KC, QKV_W, KC), BF16),
        jax.ShapeDtypeStruct((n, 3 * D_HY), BF16),
    )
    out_specs = (
        _tile_spec(QT, tm), pl.BlockSpec((tm, QKV_W), row),
        _tile_spec(KC, tm), pl.BlockSpec((tm, 3 * D_HY), row),
    )
    return pl.pallas_call(
        _proj_kernel, grid=(n // tm,), in_specs=in_specs, out_specs=out_specs,
        out_shape=out_shape, name="proj",
        compiler_params=pltpu.CompilerParams(
            dimension_semantics=("parallel",), vmem_limit_bytes=VMEM_LIMIT),
    )(x2, g, w1, why, gq, wqt, gkv, wk, wvt, vone, place, qtab, kcos, ksin)


def _attn_kernel(q_ref, k_ref, v_ref, o_ref, s0_ref, s1_ref, acc_ref, o0_ref):
    n_tiles, n_chunks = q_ref.shape[0], v_ref.shape[0]
    s_bufs = (s0_ref, s1_ref)
    sub = lax.broadcasted_iota(jnp.int32, (HEAD_PAD, QT), 0)
    units = [(t, hh) for t in range(n_tiles) for hh in range(2)]

    def feat(hh):
        return slice(hh * HEAD_PAD, (hh + 1) * HEAD_PAD)

    def stage_a(kc, unit, buf, m_run):
        t, hh = unit
        keys = pl.ds(pl.multiple_of(kc * KC, KC), KC)
        sc = _dot(k_ref[0, keys, feat(hh)], q_ref[t, feat(hh), :])
        buf[keys, :] = sc
        return jnp.maximum(m_run, jnp.max(sc.reshape(KC // 8, 8, QT), axis=0))

    def stage_b(kc, unit, buf, m):
        _, hh = unit
        keys = pl.ds(pl.multiple_of(kc * KC, KC), KC)
        p = jnp.exp2((buf[keys, :] - m).astype(BF16))
        acc_ref[...] += _dot(v_ref[kc, feat(hh), :], p)

    neg = jnp.full((8, QT), -jnp.inf, F32)
    m_run = lax.fori_loop(0, n_chunks, lambda kc, mr: stage_a(kc, units[0], s_bufs[0], mr), neg)
    for u, unit in enumerate(units):
        m = jnp.max(m_run, axis=0, keepdims=True)
        acc_ref[...] = jnp.zeros_like(acc_ref)
        cur, nxt = s_bufs[u % 2], s_bufs[(u + 1) % 2]
        if u + 1 < len(units):
            def both(kc, mr, unit=unit, nxt_unit=units[u + 1], cur=cur, nxt=nxt, m=m):
                mr = stage_a(kc, nxt_unit, nxt, mr)
                stage_b(kc, unit, cur, m)
                return mr
            m_run = lax.fori_loop(0, n_chunks, both, neg)
        else:
            def last(kc, carry, unit=unit, cur=cur, m=m):
                stage_b(kc, unit, cur, m)
                return carry
            lax.fori_loop(0, n_chunks, last, 0)
        t, hh = unit
        acc = acc_ref[...]
        if hh == 0:
            o0_ref[...] = acc / acc[V_HEAD:V_HEAD + 1, :]
        else:
            o_t = jnp.where(sub < V_HEAD, o0_ref[...], acc / acc[0:1, :])
            o_ref[0, t * QT:(t + 1) * QT, :] = o_t.T.astype(BF16)


def _attn_call(q3, k3, vt):
    b, seq, _ = k3.shape
    return pl.pallas_call(
        _attn_kernel, grid=(b, N_HEADS // 2),
        in_specs=[
            pl.BlockSpec((seq // QT, 2 * HEAD_PAD, QT), lambda i, j: (i, j, 0)),
            pl.BlockSpec((1, seq, 2 * HEAD_PAD), lambda i, j: (i, 0, j)),
            pl.BlockSpec((seq // KC, 2 * HEAD_PAD, KC), lambda i, j: (i, j, 0)),
        ],
        out_specs=pl.BlockSpec((1, seq, 2 * V_HEAD), lambda i, j: (i, 0, j)),
        out_shape=jax.ShapeDtypeStruct((b, seq, N_HEADS * V_HEAD), BF16),
        scratch_shapes=[pltpu.VMEM((seq, QT), F32), pltpu.VMEM((seq, QT), F32),
                        pltpu.VMEM((HEAD_PAD, QT), F32), pltpu.VMEM((HEAD_PAD, QT), F32)],
        name="attn",
        compiler_params=pltpu.CompilerParams(
            dimension_semantics=("parallel", "parallel"), vmem_limit_bytes=VMEM_LIMIT),
    )(q3, k3, vt)


def _hi_lo(a):
    hi = a.astype(BF16)
    return hi, (a - hi.astype(F32)).astype(BF16)


def _dot3(a, b):
    ah, al = _hi_lo(a)
    bh, bl = _hi_lo(b)
    return _dot(ah, bh) + (_dot(ah, bl) + _dot(al, bh))


def _filter_kernel(z_ref, decay_ref, w1, b1, f1, w2, b2, f2, w3, b3, f3, w4, filt_ref):
    h = jnp.sin(f1[0] * (_dot3(z_ref[...], w1[0]) + b1[0]))
    h = jnp.sin(f2[0] * (_dot3(h, w2[0]) + b2[0]))
    h = jnp.sin(f3[0] * (_dot3(h, w3[0]) + b3[0]))
    filt_ref[0] = _dot3(h, w4[0]) * (decay_ref[...] + MOD_SHIFT)


def _spectrum_kernel(f_ref, kd_ref, km_ref, gc_ref, gs_ref):
    kd = _dot3(f_ref[...], kd_ref[0])
    km = _dot3(f_ref[...], km_ref[0])
    row = lax.broadcasted_iota(jnp.int32, (HC, D_HY), 0)
    sigma = jnp.where(row % 2 == 0, 1.0, -1.0)
    gc_ref[0, 0] = (kd[:HC] + sigma * km[HC:]) * (1.0 / HC)
    gs_ref[0, 0] = (kd[HC:] - sigma * km[:HC]) * (1.0 / HC)


def _filter_spectrum(length, table, fw):
    depth = fw["w1"].shape[0]
    t = jnp.linspace(0.0, 1.0, length, dtype=F32)[:, None]
    bands = (FILTER_EMB - 1) // 2
    freqs = jnp.linspace(1e-4, bands - 1, bands, dtype=F32)[None, :]
    w = 2.0 * math.pi * jnp.arange(length, dtype=F32)[:, None] / length
    z = jnp.concatenate([t, jnp.cos(freqs * w), -jnp.sin(freqs * w)], axis=-1)
    deltas = jnp.abs(jnp.linspace(math.log(DECAY_TARGET) / FAST_DECAY_PCT,
                                  math.log(DECAY_TARGET) / SLOW_DECAY_PCT, D_HY, dtype=F32))
    decay = jnp.exp(-t * jnp.tile(deltas, 2)[None, :])

    z = jnp.pad(z, ((0, 0), (0, LANES - FILTER_EMB)))

    def lay(name):
        a = fw[name]
        if a.ndim == 2:
            a = a[:, None, :]
        rows = a.shape[1] if a.shape[1] == 1 else LANES
        cols = a.shape[2] if name == "w4" else LANES
        a = jnp.pad(a, ((0, 0), (0, rows - a.shape[1]), (0, cols - a.shape[2])))
        return a, pl.BlockSpec((1,) + a.shape[1:], lambda l: (l, 0, 0))

    names = ["w1", "b1", "f1", "w2", "b2", "f2", "w3", "b3", "f3", "w4"]
    arrs, specs = zip(*[lay(nm) for nm in names])
    filt = pl.pallas_call(
        _filter_kernel, grid=(depth,),
        in_specs=[_const_spec(z.shape), _const_spec(decay.shape), *specs],
        out_specs=pl.BlockSpec((1, length, 2 * D_HY), lambda l: (l, 0, 0)),
        out_shape=jax.ShapeDtypeStruct((depth, length, 2 * D_HY), F32), name="hyena_filter",
        compiler_params=pltpu.CompilerParams(vmem_limit_bytes=VMEM_LIMIT),
    )(z, decay, *arrs)

    fwd, bwd = filt[..., :D_HY], filt[..., D_HY:]
    k_lag = jnp.concatenate([jnp.zeros((depth, 1, D_HY), F32), jnp.flip(bwd[:, 1:], axis=1), fwd], axis=1)

    nd = 2 * (length // HC) - 1
    blk = lambda off: pl.BlockSpec((1, HC, D_HY), lambda l, d: (l, d + off, 0))
    out = pl.BlockSpec((1, 1, HC, D_HY), lambda l, d: (l, d, 0, 0))
    g = jax.ShapeDtypeStruct((depth, nd, HC, D_HY), F32)
    return pl.pallas_call(
        _spectrum_kernel, grid=(depth, nd), in_specs=[_const_spec(table.shape), blk(1), blk(0)],
        out_specs=(out, out), out_shape=(g, g), name="hyena_spectrum",
        compiler_params=pltpu.CompilerParams(vmem_limit_bytes=VMEM_LIMIT),
    )(table, k_lag, k_lag)


def _dft_table():
    f = jnp.arange(HC, dtype=jnp.int32)[:, None]
    r = jnp.arange(HC, dtype=jnp.int32)[None, :]
    ang = (((2 * f + 1) * r) % (4 * HC)).astype(F32) * (math.pi / (2 * HC))
    return jnp.concatenate([jnp.cos(ang), jnp.sin(ang)], axis=0)


def _hyena_kernel(x0_ref, x1_ref, v_ref, cw_ref, cb_ref, skip_ref, gc_ref, gs_ref, f_ref, ft_ref,
                  y_ref, zf_s, zb_s, x0_s, zc_s, zs_s, yy_s):
    length, ch = zf_s.shape
    nb = length // HC
    row = lax.broadcasted_iota(jnp.int32, (length, ch), 0)

    def short_conv(ref, role):
        u = ref[0].astype(F32)
        prev = jnp.where(row == 0, 0.0, pltpu.roll(u, 1, axis=0))
        nxt = jnp.where(row == length - 1, 0.0, pltpu.roll(u, length - 1, axis=0))
        w0, w1, w2 = (cw_ref[3 * tap + role:3 * tap + role + 1, :] for tap in range(3))
        return prev * w0 + u * w1 + nxt * w2 + cb_ref[role:role + 1, :]

    x0_s[...] = short_conv(x0_ref, 0)
    z = short_conv(v_ref, 2) * short_conv(x1_ref, 1)
    zf_s[...] = z
    zb_s[...] = z.astype(BF16)

    for j in range(nb):
        zz = _dot(f_ref[...], zb_s[j * HC:(j + 1) * HC, :])
        zc_s[j] = zz[:HC]
        zs_s[j] = zz[HC:]

    def mix(t, carry):
        r = pl.ds(pl.multiple_of(t * MIX_ROWS, MIX_ROWS), MIX_ROWS)
        r_sin = pl.ds(pl.multiple_of(HC + t * MIX_ROWS, MIX_ROWS), MIX_ROWS)
        zc = [zc_s[j, r, :] for j in range(nb)]
        zs = [zs_s[j, r, :] for j in range(nb)]
        gc = [gc_ref[0, d, r, :] for d in range(2 * nb - 1)]
        gs = [gs_ref[0, d, r, :] for d in range(2 * nb - 1)]
        for i in range(nb):
            yc = ys = None
            for j in range(nb):
                d = i - j + nb - 1
                tc = zc[j] * gc[d] - zs[j] * gs[d]
                ts = zc[j] * gs[d] + zs[j] * gc[d]
                yc = tc if yc is None else yc + tc
                ys = ts if ys is None else ys + ts
            yy_s[i, r, :] = yc.astype(BF16)
            yy_s[i, r_sin, :] = ys.astype(BF16)
        return carry

    lax.fori_loop(0, HC // MIX_ROWS, mix, 0)

    for i in range(nb):
        rows = slice(i * HC, (i + 1) * HC)
        y = _dot(ft_ref[...], yy_s[i])
        y = (y + zf_s[rows, :] * skip_ref[...]) * x0_s[rows, :]
        y_ref[0, rows, :] = y.astype(BF16)


def _hyena_call(hy3, cw, cb, skip, gc, gs, layer, table_b, table_t):
    b, length, _ = hy3.shape
    ch = D_HY // 2
    nh = D_HY // ch
    nb = length // HC

    def role_spec(role):
        return pl.BlockSpec((1, length, ch), lambda j, i: (i, 0, role * nh + j))

    g_blk = pl.BlockSpec((1, 2 * nb - 1, HC, ch), lambda j, i: (layer, 0, 0, j),
                         pipeline_mode=pl.Buffered(1))
    in_specs = [
        role_spec(0), role_spec(1), role_spec(2),
        pl.BlockSpec((9, ch), lambda j, i: (0, j)),
        pl.BlockSpec((3, ch), lambda j, i: (0, j)),
        pl.BlockSpec((1, ch), lambda j, i: (0, j)),
        g_blk, g_blk,
        _const_spec(table_b.shape), _const_spec(table_t.shape),
    ]
    return pl.pallas_call(
        _hyena_kernel, grid=(nh, b), in_specs=in_specs,
        out_specs=pl.BlockSpec((1, length, ch), lambda j, i: (i, 0, j)),
        out_shape=jax.ShapeDtypeStruct((b, length, D_HY), BF16),
        scratch_shapes=[
            pltpu.VMEM((length, ch), F32), pltpu.VMEM((length, ch), BF16),
            pltpu.VMEM((length, ch), F32), pltpu.VMEM((nb, HC, ch), F32),
            pltpu.VMEM((nb, HC, ch), F32), pltpu.VMEM((nb, 2 * HC, ch), BF16),
        ],
        name="hyena",
        compiler_params=pltpu.CompilerParams(
            dimension_semantics=("parallel", "parallel"), vmem_limit_bytes=VMEM_LIMIT),
    )(hy3, hy3, hy3, cw, cb, skip, gc, gs, table_b, table_t)


def _merge_kernel(final_norm, x_ref, ao_ref, hy_ref, gm_ref, wga_ref, wgb_ref, wap_ref, whp_ref,
                  wo_ref, gf_ref, wg_ref, wu_ref, wd_ref, gfin_ref, o_ref):
    x = x_ref[...]
    hb = _rms(x, gm_ref[...]).astype(BF16)
    merged = jax.nn.sigmoid(_dot(hb, wga_ref[...])) * _dot(ao_ref[...], wap_ref[...])
    merged = merged + jax.nn.sigmoid(_dot(hb, wgb_ref[...])) * _dot(hy_ref[...], whp_ref[...])
    x = x + _dot(merged.astype(BF16), wo_ref[...])
    hb = _rms(x, gf_ref[...]).astype(BF16)
    for start, width in FFN_CHUNKS:
        cols = slice(start, start + width)
        act = jax.nn.silu(_dot(hb, wg_ref[:, cols])) * _dot(hb, wu_ref[:, cols])
        x = x + _dot(act.astype(BF16), wd_ref[cols, :])
    if final_norm:
        x = _rms(x, gfin_ref[...])
    o_ref[...] = x


def _merge_call(x2, ao2, hy2, gm, wga, wgb, wap, whp, wo, gf, wg, wu, wd, gfin, final_norm):
    n = x2.shape[0]
    tm = TM_MERGE
    row = lambda i: (i, 0)
    consts = [gm, wga, wgb, wap, whp, wo, gf, wg, wu, wd, gfin]
    in_specs = [
        pl.BlockSpec((tm, D_MODEL), row),
        pl.BlockSpec((tm, ao2.shape[1]), row),
        pl.BlockSpec((tm, hy2.shape[1]), row),
    ] + [_const_spec(c.shape) for c in consts]
    return pl.pallas_call(
        functools.partial(_merge_kernel, final_norm), grid=(n // tm,), in_specs=in_specs,
        out_specs=pl.BlockSpec((tm, D_MODEL), row),
        out_shape=jax.ShapeDtypeStruct((n, D_MODEL), F32), name="merge_ffn",
        compiler_params=pltpu.CompilerParams(
            dimension_semantics=("parallel",), vmem_limit_bytes=VMEM_LIMIT),
    )(x2, ao2, hy2, *consts)


def _swap_pairs(w):
    w2 = w.reshape(w.shape[0], -1, 2)
    return jnp.stack([-w2[..., 1], w2[..., 0]], axis=-1).reshape(w.shape)


def _layer_weights(w_in, w_q_up, w_kv_up):
    c0, c1, c2, c3 = Q_LORA, Q_LORA + KV_LORA, Q_LORA + KV_LORA + QK_ROPE, Q_LORA + KV_LORA + QK_ROPE + 3 * D_HY
    w_kpe = w_in[:, c1:c2]
    pad = jnp.zeros((D_MODEL, P1_W - c2 - QK_ROPE), F32)
    w1 = jnp.concatenate([w_in[:, :c1], w_kpe, _swap_pairs(w_kpe), pad], axis=1).astype(BF16)
    why = w_in[:, c2:c3].astype(BF16)
    wga = w_in[:, c3:c3 + D_MODEL].astype(BF16)
    wgb = w_in[:, c3 + D_MODEL:].astype(BF16)

    wq3 = w_q_up.reshape(Q_LORA, N_HEADS, QK_NOPE + QK_ROPE)
    wq_pe = wq3[..., QK_NOPE:]
    wq = jnp.concatenate([wq3[..., :QK_NOPE], wq_pe, _swap_pairs(wq_pe.reshape(Q_LORA, -1)).reshape(wq_pe.shape)],
                         axis=-1).reshape(Q_LORA, QKV_W).astype(BF16)

    wkv3 = w_kv_up.reshape(KV_LORA, N_HEADS, QK_NOPE + V_HEAD)
    zpad = jnp.zeros((KV_LORA, N_HEADS, HEAD_PAD - QK_NOPE), F32)
    wk = jnp.concatenate([wkv3[..., :QK_NOPE], zpad], axis=-1).reshape(KV_LORA, QKV_W).astype(BF16)
    wv3 = wkv3[..., QK_NOPE:]
    odd = (jnp.arange(N_HEADS) % 2 == 1)[None, :, None]
    wv = jnp.where(odd, jnp.concatenate([zpad, wv3], axis=-1), jnp.concatenate([wv3, zpad], axis=-1))
    wv = wv.reshape(KV_LORA, QKV_W).astype(BF16)
    return w1, why, wga, wgb, wq.T, wk, wv.T


def _static_tables():
    lane = jnp.arange(QKV_W) % HEAD_PAD
    head = jnp.arange(QKV_W) // HEAD_PAD
    vone = jnp.where(head % 2 == 0, lane == V_HEAD, lane == 0).astype(F32)[:, None]
    src = jnp.arange(LANES)[:, None]
    place = ((src < QK_ROPE) & (lane[None, :] >= QK_NOPE) & ((lane[None, :] - QK_NOPE) % QK_ROPE == src))
    return vone, place.astype(BF16)


def kernel(x, positions, mix_norm_g, w_in, q_norm_g, w_q_up, kv_norm_g, w_kv_up, w_attn_proj, hy_conv_w, hy_conv_b, filt_w1, filt_b1, filt_f1, filt_w2, filt_b2, filt_f2, filt_w3, filt_b3, filt_f3, filt_w4, hy_skip, w_hy_proj, w_out, ffn_norm_g, w_gate, w_up, w_down, final_norm_g):
    b, seq, d = x.shape
    depth = w_in.shape[0]
    qtab, kcos, ksin = _rope_tables(positions)
    vone, place = _static_tables()
    table = _dft_table()
    table_b = table.astype(BF16)
    fw = dict(w1=filt_w1, b1=filt_b1, f1=filt_f1, w2=filt_w2, b2=filt_b2, f2=filt_f2,
              w3=filt_w3, b3=filt_b3, f3=filt_f3, w4=filt_w4)
    g_cos, g_sin = _filter_spectrum(seq, table, fw)

    x2 = x.reshape(b * seq, d)
    for i in range(depth):
        w1, why, wga, wgb, wqt, wk, wvt = _layer_weights(w_in[i], w_q_up[i], w_kv_up[i])
        q3, k2, vt, hy2 = _proj_call(
            x2, seq, mix_norm_g[i][None], w1, why, q_norm_g[i][None], wqt, kv_norm_g[i][None],
            wk, wvt, vone, place, qtab, kcos, ksin)
        ao = _attn_call(q3, k2.reshape(b, seq, QKV_W), vt)
        hy = _hyena_call(
            hy2.reshape(b, seq, 3 * D_HY), hy_conv_w[i].reshape(9, D_HY), hy_conv_b[i].reshape(3, D_HY),
            hy_skip[i][None], g_cos, g_sin, i, table_b, table_b.T)
        x2 = _merge_call(
            x2, ao.reshape(b * seq, -1), hy.reshape(b * seq, -1), mix_norm_g[i][None], wga, wgb,
            w_attn_proj[i].astype(BF16), w_hy_proj[i].astype(BF16), w_out[i].astype(BF16),
            ffn_norm_g[i][None], w_gate[i].astype(BF16), w_up[i].astype(BF16), w_down[i].astype(BF16),
            final_norm_g[None], final_norm=(i == depth - 1))
    return x2.reshape(b, seq, d)
```

```python
import functools
import math

import jax
import jax.numpy as jnp
from jax import lax
from jax.experimental import pallas as pl
from jax.experimental.pallas import tpu as pltpu

D_MODEL = 1024
N_HEADS = 8
Q_LORA = 256
KV_LORA = 128
QK_NOPE = 64
QK_ROPE = 32
V_HEAD = 64
ROPE_THETA = 10000.0
D_HY = 512
FILTER_EMB = 33
FILTER_HIDDEN = 64
FAST_DECAY_PCT = 0.3
SLOW_DECAY_PCT = 1.5
DECAY_TARGET = 1e-2
MOD_SHIFT = 0.0
EPS = 1e-6

LANES = 128
HEAD_PAD = LANES
QKV_W = N_HEADS * HEAD_PAD
P1_W = 512
VMEM_LIMIT = 56 * 1024 * 1024

TM_PROJ = 512
TM_MERGE = 512
QT = 512
KC = 1024
HC = 512
MIX_ROWS = 16
FFN_CHUNKS = ((0, 1536), (1536, 1280))

F32 = jnp.float32
BF16 = jnp.bfloat16


def _const_spec(shape):
    nd = len(shape)
    return pl.BlockSpec(shape, lambda *_: (0,) * nd, pipeline_mode=pl.Buffered(1))


def _rms(x, g):
    return x * lax.rsqrt(jnp.mean(x * x, axis=-1, keepdims=True) + EPS) * g


def _dot(a, b):
    return jnp.dot(a, b, preferred_element_type=F32)


def _rope_kernel(pos_col_ref, pos_row_ref, invq_ref, invk_ref, qtab_ref, kcos_ref, ksin_ref):
    sub = lax.broadcasted_iota(jnp.int32, qtab_ref.shape, 0)
    scale = (QK_NOPE + QK_ROPE) ** -0.5 * math.log2(math.e)
    angq = pos_row_ref[...].astype(F32) * invq_ref[...]
    rot = jnp.where(sub < QK_NOPE + QK_ROPE, jnp.cos(angq), jnp.sin(angq))
    qtab_ref[...] = jnp.where(sub < QK_NOPE, 1.0, rot) * scale
    lane = lax.broadcasted_iota(jnp.int32, kcos_ref.shape, 1)
    angk = pos_col_ref[...].astype(F32) * invk_ref[...]
    kcos_ref[...] = jnp.where(lane < QK_ROPE, jnp.cos(angk), 0.0)
    ksin_ref[...] = jnp.where(lane < QK_ROPE, jnp.sin(angk), 0.0)


def _rope_tables(positions):
    s = positions.shape[0]
    inv = 1.0 / (ROPE_THETA ** (jnp.arange(0, QK_ROPE, 2, dtype=F32) / QK_ROPE))
    inv_pairs = jnp.repeat(inv, 2)
    zeros = jnp.zeros((QK_NOPE,), F32)
    invq = jnp.concatenate([zeros, inv_pairs, inv_pairs])[:, None]
    invk = jnp.concatenate([inv_pairs, jnp.zeros((LANES - QK_ROPE,), F32)])[None, :]
    return pl.pallas_call(
        _rope_kernel, name="rope_tables",
        out_shape=(jax.ShapeDtypeStruct((LANES, s), F32), jax.ShapeDtypeStruct((s, LANES), F32),
                   jax.ShapeDtypeStruct((s, LANES), F32)),
    )(positions.reshape(s, 1), positions.reshape(1, s), invq, invk)


_NT = (((1,), (1,)), ((), ()))


def _store_tiles(out_ref, val):
    width = out_ref.shape[2]
    for c in range(out_ref.shape[0]):
        out_ref[c] = val[:, c * width:(c + 1) * width]


def _tile_spec(tile, tm):
    if tile >= tm:
        per = tile // tm
        return pl.BlockSpec((1, QKV_W, tm), lambda i: (i // per, 0, i % per))
    return pl.BlockSpec((tm // tile, QKV_W, tile), lambda i: (i, 0, 0))


def _proj_kernel(x_ref, g_ref, w1_ref, why_ref, gq_ref, wqt_ref, gkv_ref, wk_ref, wvt_ref,
                 vone_ref, place_ref, qtab_ref, kcos_ref, ksin_ref,
                 q_out, k_out, v_out, hy_out):
    hb = _rms(x_ref[...], g_ref[...]).astype(BF16)
    p1 = _dot(hb, w1_ref[...])
    hy_out[...] = _dot(hb, why_ref[...]).astype(BF16)

    qn = _rms(p1[:, :Q_LORA], gq_ref[...]).astype(BF16)
    qtab = jnp.concatenate([qtab_ref[...]] * N_HEADS, axis=0)
    q_t = (lax.dot_general(wqt_ref[...], qn, _NT, preferred_element_type=F32) * qtab).astype(BF16)
    _store_tiles(q_out, q_t)

    kvn = _rms(p1[:, Q_LORA:Q_LORA + KV_LORA], gkv_ref[...]).astype(BF16)
    t = p1[:, Q_LORA + KV_LORA:]
    k_rot = t * kcos_ref[...] + pltpu.roll(t, LANES - QK_ROPE, axis=1) * ksin_ref[...]
    k_out[...] = (_dot(kvn, wk_ref[...]) + _dot(k_rot.astype(BF16), place_ref[...])).astype(BF16)
    v_t = lax.dot_general(wvt_ref[...], kvn, _NT, preferred_element_type=F32) + vone_ref[...]
    _store_tiles(v_out, v_t.astype(BF16))


def _proj_call(x2, seq, g, w1, why, gq, wqt, gkv, wk, wvt, vone, place, qtab, kcos, ksin):
    n = x2.shape[0]
    tm = TM_PROJ
    tiles_per_seq = seq // tm
    row = lambda i: (i, 0)
    tab = lambda i: (i % tiles_per_seq, 0)
    in_specs = [
        pl.BlockSpec((tm, D_MODEL), row),
        _const_spec(g.shape), _const_spec(w1.shape), _const_spec(why.shape),
        _const_spec(gq.shape), _const_spec(wqt.shape), _const_spec(gkv.shape),
        _const_spec(wk.shape), _const_spec(wvt.shape), _const_spec(vone.shape),
        _const_spec(place.shape),
        pl.BlockSpec((LANES, tm), lambda i: (0, i % tiles_per_seq)),
        pl.BlockSpec((tm, LANES), tab), pl.BlockSpec((tm, LANES), tab),
    ]
    out_shape = (
        jax.ShapeDtypeStruct((n // QT, QKV_W, QT), BF16),
        jax.ShapeDtypeStruct((n, QKV_W), BF16),
        jax.ShapeDtypeStruct((n // KC, QKV_W, KC), BF16),
        jax.ShapeDtypeStruct((n, 3 * D_HY), BF16),
    )
    out_specs = (
        _tile_spec(QT, tm), pl.BlockSpec((tm, QKV_W), row),
        _tile_spec(KC, tm), pl.BlockSpec((tm, 3 * D_HY), row),
    )
    return pl.pallas_call(
        _proj_kernel, grid=(n // tm,), in_specs=in_specs, out_specs=out_specs,
        out_shape=out_shape, name="proj",
        compiler_params=pltpu.CompilerParams(
            dimension_semantics=("parallel",), vmem_limit_bytes=VMEM_LIMIT),
    )(x2, g, w1, why, gq, wqt, gkv, wk, wvt, vone, place, qtab, kcos, ksin)


def _attn_kernel(q_ref, k_ref, v_ref, o_ref, s_ref, acc_ref):
    n_tiles, n_chunks = q_ref.shape[0], v_ref.shape[0]
    sub = lax.broadcasted_iota(jnp.int32, (HEAD_PAD, QT), 0)

    def feat(hh):
        return slice(hh * HEAD_PAD, (hh + 1) * HEAD_PAD)

    def stage_a(kc, t, m_run):
        keys = pl.ds(pl.multiple_of(kc * KC, KC), KC)
        out = []
        for hh in range(2):
            sc = _dot(k_ref[0, keys, feat(hh)], q_ref[t, feat(hh), :])
            s_ref[t % 2, hh, keys, :] = sc
            out.append(jnp.maximum(m_run[hh], jnp.max(sc.reshape(KC // 8, 8, QT), axis=0)))
        return tuple(out)

    def stage_b(kc, t, m):
        keys = pl.ds(pl.multiple_of(kc * KC, KC), KC)
        for hh in range(2):
            p = jnp.exp2((s_ref[t % 2, hh, keys, :] - m[hh]).astype(BF16))
            acc_ref[hh] += _dot(v_ref[kc, feat(hh), :], p)

    neg = (jnp.full((8, QT), -jnp.inf, F32),) * 2
    m_run = lax.fori_loop(0, n_chunks, lambda kc, mr: stage_a(kc, 0, mr), neg)
    for t in range(n_tiles):
        m = tuple(jnp.max(mr, axis=0, keepdims=True) for mr in m_run)
        acc_ref[...] = jnp.zeros_like(acc_ref)
        if t + 1 < n_tiles:
            def both(kc, mr, t=t, m=m):
                mr = stage_a(kc, t + 1, mr)
                stage_b(kc, t, m)
                return mr
            m_run = lax.fori_loop(0, n_chunks, both, neg)
        else:
            def last(kc, carry, t=t, m=m):
                stage_b(kc, t, m)
                return carry
            lax.fori_loop(0, n_chunks, last, 0)
        acc0, acc1 = acc_ref[0], acc_ref[1]
        o_t = jnp.where(sub < V_HEAD, acc0 / acc0[V_HEAD:V_HEAD + 1, :], acc1 / acc1[0:1, :])
        o_ref[0, t * QT:(t + 1) * QT, :] = o_t.T.astype(BF16)


def _attn_call(q3, k3, vt):
    b, seq, _ = k3.shape
    return pl.pallas_call(
        _attn_kernel, grid=(b, N_HEADS // 2),
        in_specs=[
            pl.BlockSpec((seq // QT, 2 * HEAD_PAD, QT), lambda i, j: (i, j, 0)),
            pl.BlockSpec((1, seq, 2 * HEAD_PAD), lambda i, j: (i, 0, j)),
            pl.BlockSpec((seq // KC, 2 * HEAD_PAD, KC), lambda i, j: (i, j, 0)),
        ],
        out_specs=pl.BlockSpec((1, seq, 2 * V_HEAD), lambda i, j: (i, 0, j)),
        out_shape=jax.ShapeDtypeStruct((b, seq, N_HEADS * V_HEAD), BF16),
        scratch_shapes=[pltpu.VMEM((2, 2, seq, QT), F32), pltpu.VMEM((2, HEAD_PAD, QT), F32)],
        name="attn",
        compiler_params=pltpu.CompilerParams(
            dimension_semantics=("parallel", "parallel"), vmem_limit_bytes=VMEM_LIMIT),
    )(q3, k3, vt)


def _hi_lo(a):
    hi = a.astype(BF16)
    return hi, (a - hi.astype(F32)).astype(BF16)


def _dot3(a, b):
    ah, al = _hi_lo(a)
    bh, bl = _hi_lo(b)
    return _dot(ah, bh) + (_dot(ah, bl) + _dot(al, bh))


def _filter_kernel(z_ref, decay_ref, w1, b1, f1, w2, b2, f2, w3, b3, f3, w4, filt_ref):
    h = jnp.sin(f1[0] * (_dot3(z_ref[...], w1[0]) + b1[0]))
    h = jnp.sin(f2[0] * (_dot3(h, w2[0]) + b2[0]))
    h = jnp.sin(f3[0] * (_dot3(h, w3[0]) + b3[0]))
    filt_ref[0] = _dot3(h, w4[0]) * decay_ref[...]


def _spectrum_kernel(f_ref, kd_ref, km_ref, gc_ref, gs_ref):
    kd = _dot3(f_ref[...], kd_ref[0])
    km = _dot3(f_ref[...], km_ref[0])
    row = lax.broadcasted_iota(jnp.int32, (HC, D_HY), 0)
    sigma = jnp.where(row % 2 == 0, 1.0, -1.0)
    gc_ref[0, 0] = (kd[:HC] + sigma * km[HC:]) * (1.0 / HC)
    gs_ref[0, 0] = (kd[HC:] - sigma * km[:HC]) * (1.0 / HC)


def _filter_spectrum(length, table, fw):
    depth = fw["w1"].shape[0]
    t = jnp.linspace(0.0, 1.0, length, dtype=F32)[:, None]
    bands = (FILTER_EMB - 1) // 2
    freqs = jnp.linspace(1e-4, bands - 1, bands, dtype=F32)[None, :]
    w = 2.0 * math.pi * jnp.arange(length, dtype=F32)[:, None] / length
    z = jnp.concatenate([t, jnp.cos(freqs * w), -jnp.sin(freqs * w)], axis=-1)
    deltas = jnp.abs(jnp.linspace(math.log(DECAY_TARGET) / FAST_DECAY_PCT,
                                  math.log(DECAY_TARGET) / SLOW_DECAY_PCT, D_HY, dtype=F32))
    decay = jnp.exp(-t * jnp.tile(deltas, 2)[None, :]) + MOD_SHIFT

    half = LANES // 2
    rev = lambda a: jnp.concatenate([a[:1], jnp.flip(a[1:], axis=0)], axis=0)
    zp = jnp.pad(z, ((0, 0), (0, half - FILTER_EMB)))
    z2 = jnp.concatenate([rev(zp), zp], axis=1)
    decay2 = jnp.concatenate([rev(decay[:, D_HY:]).at[0].set(0.0), decay[:, :D_HY]], axis=1)

    def lay(name):
        a = fw[name]
        if a.ndim == 2:
            a = jnp.pad(a, ((0, 0), (0, half - a.shape[1])))
            a = jnp.concatenate([a, a], axis=1)[:, None, :]
        elif name == "w4":
            zero = jnp.zeros_like(a[..., :D_HY])
            a = jnp.concatenate([jnp.concatenate([a[..., D_HY:], zero], axis=2),
                                 jnp.concatenate([zero, a[..., :D_HY]], axis=2)], axis=1)
        else:
            a = jnp.pad(a, ((0, 0), (0, half - a.shape[1]), (0, half - a.shape[2])))
            zero = jnp.zeros_like(a)
            a = jnp.concatenate([jnp.concatenate([a, zero], axis=2), jnp.concatenate([zero, a], axis=2)], axis=1)
        return a, pl.BlockSpec((1,) + a.shape[1:], lambda l: (l, 0, 0))

    names = ["w1", "b1", "f1", "w2", "b2", "f2", "w3", "b3", "f3", "w4"]
    arrs, specs = zip(*[lay(nm) for nm in names])
    k_lag = pl.pallas_call(
        _filter_kernel, grid=(depth,),
        in_specs=[_const_spec(z2.shape), _const_spec(decay2.shape), *specs],
        out_specs=pl.BlockSpec((1, length, 2 * D_HY), lambda l: (l, 0, 0)),
        out_shape=jax.ShapeDtypeStruct((depth, length, 2 * D_HY), F32), name="hyena_filter",
        compiler_params=pltpu.CompilerParams(vmem_limit_bytes=VMEM_LIMIT),
    )(z2, decay2, *arrs)

    nb = length // HC
    nd = 2 * nb - 1
    blk = lambda off: pl.BlockSpec((1, HC, D_HY), lambda l, d: (l, (d + off) % nb, (d + off) // nb))
    out = pl.BlockSpec((1, 1, HC, D_HY), lambda l, d: (l, d, 0, 0))
    g = jax.ShapeDtypeStruct((depth, nd, HC, D_HY), F32)
    return pl.pallas_call(
        _spectrum_kernel, grid=(depth, nd), in_specs=[_const_spec(table.shape), blk(1), blk(0)],
        out_specs=(out, out), out_shape=(g, g), name="hyena_spectrum",
        compiler_params=pltpu.CompilerParams(vmem_limit_bytes=VMEM_LIMIT),
    )(table, k_lag, k_lag)


def _dft_table():
    f = jnp.arange(HC, dtype=jnp.int32)[:, None]
    r = jnp.arange(HC, dtype=jnp.int32)[None, :]
    ang = (((2 * f + 1) * r) % (4 * HC)).astype(F32) * (math.pi / (2 * HC))
    return jnp.concatenate([jnp.cos(ang), jnp.sin(ang)], axis=0)


def _hyena_kernel(x0_ref, x1_ref, v_ref, cw_ref, cb_ref, skip_ref, gc_ref, gs_ref, f_ref, ft_ref,
                  y_ref, zf_s, zb_s, x0_s, zc_s, zs_s, yy_s):
    length, ch = zf_s.shape
    nb = length // HC
    row = lax.broadcasted_iota(jnp.int32, (length, ch), 0)

    def short_conv(ref, role):
        u = ref[0].astype(F32)
        prev = jnp.where(row == 0, 0.0, pltpu.roll(u, 1, axis=0))
        nxt = jnp.where(row == length - 1, 0.0, pltpu.roll(u, length - 1, axis=0))
        w0, w1, w2 = (cw_ref[3 * tap + role:3 * tap + role + 1, :] for tap in range(3))
        return prev * w0 + u * w1 + nxt * w2 + cb_ref[role:role + 1, :]

    x0_s[...] = short_conv(x0_ref, 0)
    z = short_conv(v_ref, 2) * short_conv(x1_ref, 1)
    zf_s[...] = z
    zb_s[...] = z.astype(BF16)

    for j in range(nb):
        zz = _dot(f_ref[...], zb_s[j * HC:(j + 1) * HC, :])
        zc_s[j] = zz[:HC]
        zs_s[j] = zz[HC:]

    def mix(t, carry):
        r = pl.ds(pl.multiple_of(t * MIX_ROWS, MIX_ROWS), MIX_ROWS)
        r_sin = pl.ds(pl.multiple_of(HC + t * MIX_ROWS, MIX_ROWS), MIX_ROWS)
        zc = [zc_s[j, r, :] for j in range(nb)]
        zs = [zs_s[j, r, :] for j in range(nb)]
        gc = [gc_ref[0, d, r, :] for d in range(2 * nb - 1)]
        gs = [gs_ref[0, d, r, :] for d in range(2 * nb - 1)]
        for i in range(nb):
            yc = ys = None
            for j in range(nb):
                d = i - j + nb - 1
                tc = zc[j] * gc[d] - zs[j] * gs[d]
                ts = zc[j] * gs[d] + zs[j] * gc[d]
                yc = tc if yc is None else yc + tc
                ys = ts if ys is None else ys + ts
            yy_s[i, r, :] = yc.astype(BF16)
            yy_s[i, r_sin, :] = ys.astype(BF16)
        return carry

    lax.fori_loop(0, HC // MIX_ROWS, mix, 0)

    for i in range(nb):
        rows = slice(i * HC, (i + 1) * HC)
        y = _dot(ft_ref[...], yy_s[i])
        y = (y + zf_s[rows, :] * skip_ref[...]) * x0_s[rows, :]
        y_ref[0, rows, :] = y.astype(BF16)


def _hyena_call(hy3, cw, cb, skip, gc, gs, layer, table_b, table_t):
    b, length, _ = hy3.shape
    ch = D_HY // 2
    nh = D_HY // ch
    nb = length // HC

    def role_spec(role):
        return pl.BlockSpec((1, length, ch), lambda j, i: (i, 0, role * nh + j))

    g_blk = pl.BlockSpec((1, 2 * nb - 1, HC, ch), lambda j, i: (layer, 0, 0, j),
                         pipeline_mode=pl.Buffered(1))
    in_specs = [
        role_spec(0), role_spec(1), role_spec(2),
        pl.BlockSpec((9, ch), lambda j, i: (0, j)),
        pl.BlockSpec((3, ch), lambda j, i: (0, j)),
        pl.BlockSpec((1, ch), lambda j, i: (0, j)),
        g_blk, g_blk,
        _const_spec(table_b.shape), _const_spec(table_t.shape),
    ]
    return pl.pallas_call(
        _hyena_kernel, grid=(nh, b), in_specs=in_specs,
        out_specs=pl.BlockSpec((1, length, ch), lambda j, i: (i, 0, j)),
        out_shape=jax.ShapeDtypeStruct((b, length, D_HY), BF16),
        scratch_shapes=[
            pltpu.VMEM((length, ch), F32), pltpu.VMEM((length, ch), BF16),
            pltpu.VMEM((length, ch), F32), pltpu.VMEM((nb, HC, ch), F32),
            pltpu.VMEM((nb, HC, ch), F32), pltpu.VMEM((nb, 2 * HC, ch), BF16),
        ],
        name="hyena",
        compiler_params=pltpu.CompilerParams(
            dimension_semantics=("parallel", "parallel"), vmem_limit_bytes=VMEM_LIMIT),
    )(hy3, hy3, hy3, cw, cb, skip, gc, gs, table_b, table_t)


def _merge_kernel(final_norm, x_ref, ao_ref, hy_ref, gm_ref, wga_ref, wgb_ref, wap_ref, whp_ref,
                  wo_ref, gf_ref, wg_ref, wu_ref, wd_ref, gfin_ref, o_ref):
    x = x_ref[...]
    hb = _rms(x, gm_ref[...]).astype(BF16)
    merged = jax.nn.sigmoid(_dot(hb, wga_ref[...])) * _dot(ao_ref[...], wap_ref[...])
    merged = merged + jax.nn.sigmoid(_dot(hb, wgb_ref[...])) * _dot(hy_ref[...], whp_ref[...])
    x = x + _dot(merged.astype(BF16), wo_ref[...])
    hb = _rms(x, gf_ref[...]).astype(BF16)
    for start, width in FFN_CHUNKS:
        cols = slice(start, start + width)
        act = jax.nn.silu(_dot(hb, wg_ref[:, cols])) * _dot(hb, wu_ref[:, cols])
        x = x + _dot(act.astype(BF16), wd_ref[cols, :])
    if final_norm:
        x = _rms(x, gfin_ref[...])
    o_ref[...] = x


def _merge_call(x2, ao2, hy2, gm, wga, wgb, wap, whp, wo, gf, wg, wu, wd, gfin, final_norm):
    n = x2.shape[0]
    tm = TM_MERGE
    row = lambda i: (i, 0)
    consts = [gm, wga, wgb, wap, whp, wo, gf, wg, wu, wd, gfin]
    in_specs = [
        pl.BlockSpec((tm, D_MODEL), row),
        pl.BlockSpec((tm, ao2.shape[1]), row),
        pl.BlockSpec((tm, hy2.shape[1]), row),
    ] + [_const_spec(c.shape) for c in consts]
    return pl.pallas_call(
        functools.partial(_merge_kernel, final_norm), grid=(n // tm,), in_specs=in_specs,
        out_specs=pl.BlockSpec((tm, D_MODEL), row),
        out_shape=jax.ShapeDtypeStruct((n, D_MODEL), F32), name="merge_ffn",
        compiler_params=pltpu.CompilerParams(
            dimension_semantics=("parallel",), vmem_limit_bytes=VMEM_LIMIT),
    )(x2, ao2, hy2, *consts)


def _swap_pairs(w):
    w2 = w.reshape(w.shape[0], -1, 2)
    return jnp.stack([-w2[..., 1], w2[..., 0]], axis=-1).reshape(w.shape)


def _layer_weights(w_in, w_q_up, w_kv_up):
    c0, c1, c2, c3 = Q_LORA, Q_LORA + KV_LORA, Q_LORA + KV_LORA + QK_ROPE, Q_LORA + KV_LORA + QK_ROPE + 3 * D_HY
    w_kpe = w_in[:, c1:c2]
    pad = jnp.zeros((D_MODEL, P1_W - c2 - QK_ROPE), F32)
    w1 = jnp.concatenate([w_in[:, :c1], w_kpe, _swap_pairs(w_kpe), pad], axis=1).astype(BF16)
    why = w_in[:, c2:c3].astype(BF16)
    wga = w_in[:, c3:c3 + D_MODEL].astype(BF16)
    wgb = w_in[:, c3 + D_MODEL:].astype(BF16)

    wq3 = w_q_up.reshape(Q_LORA, N_HEADS, QK_NOPE + QK_ROPE)
    wq_pe = wq3[..., QK_NOPE:]
    wq = jnp.concatenate([wq3[..., :QK_NOPE], wq_pe, _swap_pairs(wq_pe.reshape(Q_LORA, -1)).reshape(wq_pe.shape)],
                         axis=-1).reshape(Q_LORA, QKV_W).astype(BF16)

    wkv3 = w_kv_up.reshape(KV_LORA, N_HEADS, QK_NOPE + V_HEAD)
    zpad = jnp.zeros((KV_LORA, N_HEADS, HEAD_PAD - QK_NOPE), F32)
    wk = jnp.concatenate([wkv3[..., :QK_NOPE], zpad], axis=-1).reshape(KV_LORA, QKV_W).astype(BF16)
    wv3 = wkv3[..., QK_NOPE:]
    odd = (jnp.arange(N_HEADS) % 2 == 1)[None, :, None]
    wv = jnp.where(odd, jnp.concatenate([zpad, wv3], axis=-1), jnp.concatenate([wv3, zpad], axis=-1))
    wv = wv.reshape(KV_LORA, QKV_W).astype(BF16)
    return w1, why, wga, wgb, wq.T, wk, wv.T


def _static_tables():
    lane = jnp.arange(QKV_W) % HEAD_PAD
    head = jnp.arange(QKV_W) // HEAD_PAD
    vone = jnp.where(head % 2 == 0, lane == V_HEAD, lane == 0).astype(F32)[:, None]
    src = jnp.arange(LANES)[:, None]
    place = ((src < QK_ROPE) & (lane[None, :] >= QK_NOPE) & ((lane[None, :] - QK_NOPE) % QK_ROPE == src))
    return vone, place.astype(BF16)


def kernel(x, positions, mix_norm_g, w_in, q_norm_g, w_q_up, kv_norm_g, w_kv_up, w_attn_proj, hy_conv_w, hy_conv_b, filt_w1, filt_b1, filt_f1, filt_w2, filt_b2, filt_f2, filt_w3, filt_b3, filt_f3, filt_w4, hy_skip, w_hy_proj, w_out, ffn_norm_g, w_gate, w_up, w_down, final_norm_g):
    b, seq, d = x.shape
    depth = w_in.shape[0]
    qtab, kcos, ksin = _rope_tables(positions)
    vone, place = _static_tables()
    table = _dft_table()
    table_b = table.astype(BF16)
    fw = dict(w1=filt_w1, b1=filt_b1, f1=filt_f1, w2=filt_w2, b2=filt_b2, f2=filt_f2,
              w3=filt_w3, b3=filt_b3, f3=filt_f3, w4=filt_w4)
    g_cos, g_sin = _filter_spectrum(seq, table, fw)

    x2 = x.reshape(b * seq, d)
    for i in range(depth):
        w1, why, wga, wgb, wqt, wk, wvt = _layer_weights(w_in[i], w_q_up[i], w_kv_up[i])
        q3, k2, vt, hy2 = _proj_call(
            x2, seq, mix_norm_g[i][None], w1, why, q_norm_g[i][None], wqt, kv_norm_g[i][None],
            wk, wvt, vone, place, qtab, kcos, ksin)
        ao = _attn_call(q3, k2.reshape(b, seq, QKV_W), vt)
        hy = _hyena_call(
            hy2.reshape(b, seq, 3 * D_HY), hy_conv_w[i].reshape(9, D_HY), hy_conv_b[i].reshape(3, D_HY),
            hy_skip[i][None], g_cos, g_sin, i, table_b, table_b.T)
        x2 = _merge_call(
            x2, ao.reshape(b * seq, -1), hy.reshape(b * seq, -1), mix_norm_g[i][None], wga, wgb,
            w_attn_proj[i].astype(BF16), w_hy_proj[i].astype(BF16), w_out[i].astype(BF16),
            ffn_norm_g[i][None], w_gate[i].astype(BF16), w_up[i].astype(BF16), w_down[i].astype(BF16),
            final_norm_g[None], final_norm=(i == depth - 1))
    return x2.reshape(b, seq, d)
```

```python
import functools
import math

import jax
import jax.numpy as jnp
from jax import lax
from jax.experimental import pallas as pl
from jax.experimental.pallas import tpu as pltpu

D_MODEL = 1024
N_HEADS = 8
Q_LORA = 256
KV_LORA = 128
QK_NOPE = 64
QK_ROPE = 32
V_HEAD = 64
ROPE_THETA = 10000.0
D_HY = 512
FILTER_EMB = 33
FILTER_HIDDEN = 64
FAST_DECAY_PCT = 0.3
SLOW_DECAY_PCT = 1.5
DECAY_TARGET = 1e-2
MOD_SHIFT = 0.0
EPS = 1e-6

LANES = 128
HEAD_PAD = LANES
QKV_W = N_HEADS * HEAD_PAD
P1_W = 512
VMEM_LIMIT = 56 * 1024 * 1024

TM_PROJ = 512
TM_MERGE = 512
QT = 512
KC = 1024
SUB = 512
HC = 512
MIX_ROWS = 16
FFN_CHUNKS = ((0, 1536), (1536, 1280))

F32 = jnp.float32
BF16 = jnp.bfloat16


def _const_spec(shape):
    nd = len(shape)
    return pl.BlockSpec(shape, lambda *_: (0,) * nd, pipeline_mode=pl.Buffered(1))


def _rms(x, g):
    return x * lax.rsqrt(jnp.mean(x * x, axis=-1, keepdims=True) + EPS) * g


def _dot(a, b):
    return jnp.dot(a, b, preferred_element_type=F32)


def _rope_kernel(pos_col_ref, pos_row_ref, invq_ref, invk_ref, qtab_ref, kcos_ref, ksin_ref):
    sub = lax.broadcasted_iota(jnp.int32, qtab_ref.shape, 0)
    scale = (QK_NOPE + QK_ROPE) ** -0.5 * math.log2(math.e)
    angq = pos_row_ref[...].astype(F32) * invq_ref[...]
    rot = jnp.where(sub < QK_NOPE + QK_ROPE, jnp.cos(angq), jnp.sin(angq))
    qtab_ref[...] = jnp.where(sub < QK_NOPE, 1.0, rot) * scale
    lane = lax.broadcasted_iota(jnp.int32, kcos_ref.shape, 1)
    angk = pos_col_ref[...].astype(F32) * invk_ref[...]
    kcos_ref[...] = jnp.where(lane < QK_ROPE, jnp.cos(angk), 0.0)
    ksin_ref[...] = jnp.where(lane < QK_ROPE, jnp.sin(angk), 0.0)


def _rope_tables(positions):
    s = positions.shape[0]
    inv = 1.0 / (ROPE_THETA ** (jnp.arange(0, QK_ROPE, 2, dtype=F32) / QK_ROPE))
    inv_pairs = jnp.repeat(inv, 2)
    zeros = jnp.zeros((QK_NOPE,), F32)
    invq = jnp.concatenate([zeros, inv_pairs, inv_pairs])[:, None]
    invk = jnp.concatenate([inv_pairs, jnp.zeros((LANES - QK_ROPE,), F32)])[None, :]
    return pl.pallas_call(
        _rope_kernel, name="rope_tables",
        out_shape=(jax.ShapeDtypeStruct((LANES, s), F32), jax.ShapeDtypeStruct((s, LANES), F32),
                   jax.ShapeDtypeStruct((s, LANES), F32)),
    )(positions.reshape(s, 1), positions.reshape(1, s), invq, invk)


_NT = (((1,), (1,)), ((), ()))


def _store_tiles(out_ref, val):
    width = out_ref.shape[2]
    for c in range(out_ref.shape[0]):
        out_ref[c] = val[:, c * width:(c + 1) * width]


def _tile_spec(tile, tm):
    if tile >= tm:
        per = tile // tm
        return pl.BlockSpec((1, QKV_W, tm), lambda i: (i // per, 0, i % per))
    return pl.BlockSpec((tm // tile, QKV_W, tile), lambda i: (i, 0, 0))


def _proj_kernel(x_ref, g_ref, w1_ref, why_ref, gq_ref, wqt_ref, gkv_ref, wk_ref, wvt_ref,
                 vone_ref, place_ref, qtab_ref, kcos_ref, ksin_ref,
                 q_out, k_out, v_out, hy_out):
    hb = _rms(x_ref[...], g_ref[...]).astype(BF16)
    p1 = _dot(hb, w1_ref[...])
    hy_out[...] = _dot(hb, why_ref[...]).astype(BF16)

    qn = _rms(p1[:, :Q_LORA], gq_ref[...]).astype(BF16)
    qtab = jnp.concatenate([qtab_ref[...]] * N_HEADS, axis=0)
    q_t = (lax.dot_general(wqt_ref[...], qn, _NT, preferred_element_type=F32) * qtab).astype(BF16)
    _store_tiles(q_out, q_t)

    kvn = _rms(p1[:, Q_LORA:Q_LORA + KV_LORA], gkv_ref[...]).astype(BF16)
    t = p1[:, Q_LORA + KV_LORA:]
    k_rot = t * kcos_ref[...] + pltpu.roll(t, LANES - QK_ROPE, axis=1) * ksin_ref[...]
    k_out[...] = (_dot(kvn, wk_ref[...]) + _dot(k_rot.astype(BF16), place_ref[...])).astype(BF16)
    v_t = lax.dot_general(wvt_ref[...], kvn, _NT, preferred_element_type=F32) + vone_ref[...]
    _store_tiles(v_out, v_t.astype(BF16))


def _proj_call(x2, seq, g, w1, why, gq, wqt, gkv, wk, wvt, vone, place, qtab, kcos, ksin):
    n = x2.shape[0]
    tm = TM_PROJ
    tiles_per_seq = seq // tm
    row = lambda i: (i, 0)
    tab = lambda i: (i % tiles_per_seq, 0)
    in_specs = [
        pl.BlockSpec((tm, D_MODEL), row),
        _const_spec(g.shape), _const_spec(w1.shape), _const_spec(why.shape),
        _const_spec(gq.shape), _const_spec(wqt.shape), _const_spec(gkv.shape),
        _const_spec(wk.shape), _const_spec(wvt.shape), _const_spec(vone.shape),
        _const_spec(place.shape),
        pl.BlockSpec((LANES, tm), lambda i: (0, i % tiles_per_seq)),
        pl.BlockSpec((tm, LANES), tab), pl.BlockSpec((tm, LANES), tab),
    ]
    out_shape = (
        jax.ShapeDtypeStruct((n // QT, QKV_W, QT), BF16),
        jax.ShapeDtypeStruct((n, QKV_W), BF16),
        jax.ShapeDtypeStruct((n // KC, QKV_W, KC), BF16),
        jax.ShapeDtypeStruct((n, 3 * D_HY), BF16),
    )
    out_specs = (
        _tile_spec(QT, tm), pl.BlockSpec((tm, QKV_W), row),
        _tile_spec(KC, tm), pl.BlockSpec((tm, 3 * D_HY), row),
    )
    return pl.pallas_call(
        _proj_kernel, grid=(n // tm,), in_specs=in_specs, out_specs=out_specs,
        out_shape=out_shape, name="proj",
        compiler_params=pltpu.CompilerParams(
            dimension_semantics=("parallel",), vmem_limit_bytes=VMEM_LIMIT),
    )(x2, g, w1, why, gq, wqt, gkv, wk, wvt, vone, place, qtab, kcos, ksin)


def _attn_kernel(q_ref, k_ref, v_ref, o_ref, s_ref, acc_ref):
    n_tiles, n_chunks = q_ref.shape[0], v_ref.shape[0]
    sub = lax.broadcasted_iota(jnp.int32, (HEAD_PAD, QT), 0)

    def feat(hh):
        return slice(hh * HEAD_PAD, (hh + 1) * HEAD_PAD)

    def stage_a(kc, sub_i, hh, t, m_run):
        keys = pl.ds(pl.multiple_of(kc * KC + sub_i * SUB, SUB), SUB)
        sc = _dot(k_ref[0, keys, feat(hh)], q_ref[t, feat(hh), :])
        s_ref[t % 2, hh, keys, :] = sc
        return jnp.maximum(m_run, jnp.max(sc.reshape(SUB // 8, 8, QT), axis=0))

    def stage_b(kc, sub_i, hh, t, m):
        keys = pl.ds(pl.multiple_of(kc * KC + sub_i * SUB, SUB), SUB)
        p = jnp.exp2((s_ref[t % 2, hh, keys, :] - m).astype(BF16))
        acc_ref[hh] += _dot(v_ref[kc, feat(hh), sub_i * SUB:(sub_i + 1) * SUB], p)

    def chunk(kc, t_a, t_b, m_run, m):
        m_run = list(m_run)
        for sub_i in range(KC // SUB):
            for hh in range(2):
                if t_a is not None:
                    m_run[hh] = stage_a(kc, sub_i, hh, t_a, m_run[hh])
                if t_b is not None:
                    stage_b(kc, sub_i, hh, t_b, m[hh])
        return tuple(m_run)

    neg = (jnp.full((8, QT), -jnp.inf, F32),) * 2
    m_run = lax.fori_loop(0, n_chunks, lambda kc, mr: chunk(kc, 0, None, mr, None), neg)
    for t in range(n_tiles):
        m = tuple(jnp.max(mr, axis=0, keepdims=True) for mr in m_run)
        acc_ref[...] = jnp.zeros_like(acc_ref)
        if t + 1 < n_tiles:
            def both(kc, mr, t=t, m=m):
                return chunk(kc, t + 1, t, mr, m)
            m_run = lax.fori_loop(0, n_chunks, both, neg)
        else:
            def last(kc, carry, t=t, m=m):
                chunk(kc, None, t, (), m)
                return carry
            lax.fori_loop(0, n_chunks, last, 0)
        acc0, acc1 = acc_ref[0], acc_ref[1]
        o_t = jnp.where(sub < V_HEAD, acc0 / acc0[V_HEAD:V_HEAD + 1, :], acc1 / acc1[0:1, :])
        o_ref[0, t * QT:(t + 1) * QT, :] = o_t.T.astype(BF16)


def _attn_call(q3, k3, vt):
    b, seq, _ = k3.shape
    return pl.pallas_call(
        _attn_kernel, grid=(b, N_HEADS // 2),
        in_specs=[
            pl.BlockSpec((seq // QT, 2 * HEAD_PAD, QT), lambda i, j: (i, j, 0)),
            pl.BlockSpec((1, seq, 2 * HEAD_PAD), lambda i, j: (i, 0, j)),
            pl.BlockSpec((seq // KC, 2 * HEAD_PAD, KC), lambda i, j: (i, j, 0)),
        ],
        out_specs=pl.BlockSpec((1, seq, 2 * V_HEAD), lambda i, j: (i, 0, j)),
        out_shape=jax.ShapeDtypeStruct((b, seq, N_HEADS * V_HEAD), BF16),
        scratch_shapes=[pltpu.VMEM((2, 2, seq, QT), F32), pltpu.VMEM((2, HEAD_PAD, QT), F32)],
        name="attn",
        compiler_params=pltpu.CompilerParams(
            dimension_semantics=("parallel", "parallel"), vmem_limit_bytes=VMEM_LIMIT),
    )(q3, k3, vt)


def _hi_lo(a):
    hi = a.astype(BF16)
    return hi, (a - hi.astype(F32)).astype(BF16)


def _dot3(a, b):
    ah, al = _hi_lo(a)
    bh, bl = _hi_lo(b)
    return _dot(ah, bh) + (_dot(ah, bl) + _dot(al, bh))


def _filter_kernel(z_ref, decay_ref, w1, b1, f1, w2, b2, f2, w3, b3, f3, w4, filt_ref):
    h = jnp.sin(f1[0] * (_dot3(z_ref[...], w1[0]) + b1[0]))
    h = jnp.sin(f2[0] * (_dot3(h, w2[0]) + b2[0]))
    h = jnp.sin(f3[0] * (_dot3(h, w3[0]) + b3[0]))
    filt_ref[0] = _dot3(h, w4[0]) * decay_ref[...]


def _spectrum_kernel(f_ref, kd_ref, km_ref, gc_ref, gs_ref):
    kd = _dot3(f_ref[...], kd_ref[0])
    km = _dot3(f_ref[...], km_ref[0])
    row = lax.broadcasted_iota(jnp.int32, (HC, D_HY), 0)
    sigma = jnp.where(row % 2 == 0, 1.0, -1.0)
    gc_ref[0, 0] = (kd[:HC] + sigma * km[HC:]) * (1.0 / HC)
    gs_ref[0, 0] = (kd[HC:] - sigma * km[:HC]) * (1.0 / HC)


def _filter_spectrum(length, table, fw):
    depth = fw["w1"].shape[0]
    t = jnp.linspace(0.0, 1.0, length, dtype=F32)[:, None]
    bands = (FILTER_EMB - 1) // 2
    freqs = jnp.linspace(1e-4, bands - 1, bands, dtype=F32)[None, :]
    w = 2.0 * math.pi * jnp.arange(length, dtype=F32)[:, None] / length
    deltas = jnp.abs(jnp.linspace(math.log(DECAY_TARGET) / FAST_DECAY_PCT,
                                  math.log(DECAY_TARGET) / SLOW_DECAY_PCT, D_HY, dtype=F32))[None, :]

    half = LANES // 2
    m_idx = jnp.arange(length)
    pos = (length - m_idx) % length

    def features(tt, ww):
        zf = jnp.concatenate([tt, jnp.cos(freqs * ww), -jnp.sin(freqs * ww)], axis=-1)
        return jnp.pad(zf, ((0, 0), (0, half - FILTER_EMB)))

    z2 = jnp.concatenate([features(t[pos], w[pos]), features(t, w)], axis=1)
    decay_bwd = jnp.where(m_idx[:, None] == 0, 0.0, jnp.exp(-t[pos] * deltas) + MOD_SHIFT)
    decay2 = jnp.concatenate([decay_bwd, jnp.exp(-t * deltas) + MOD_SHIFT], axis=1)

    def lay(name):
        a = fw[name]
        if a.ndim == 2:
            a = jnp.pad(a, ((0, 0), (0, half - a.shape[1])))
            a = jnp.concatenate([a, a], axis=1)[:, None, :]
        elif name == "w4":
            zero = jnp.zeros_like(a[..., :D_HY])
            a = jnp.concatenate([jnp.concatenate([a[..., D_HY:], zero], axis=2),
                                 jnp.concatenate([zero, a[..., :D_HY]], axis=2)], axis=1)
        else:
            a = jnp.pad(a, ((0, 0), (0, half - a.shape[1]), (0, half - a.shape[2])))
            zero = jnp.zeros_like(a)
            a = jnp.concatenate([jnp.concatenate([a, zero], axis=2), jnp.concatenate([zero, a], axis=2)], axis=1)
        return a, pl.BlockSpec((1,) + a.shape[1:], lambda l: (l, 0, 0))

    names = ["w1", "b1", "f1", "w2", "b2", "f2", "w3", "b3", "f3", "w4"]
    arrs, specs = zip(*[lay(nm) for nm in names])
    k_lag = pl.pallas_call(
        _filter_kernel, grid=(depth,),
        in_specs=[_const_spec(z2.shape), _const_spec(decay2.shape), *specs],
        out_specs=pl.BlockSpec((1, length, 2 * D_HY), lambda l: (l, 0, 0)),
        out_shape=jax.ShapeDtypeStruct((depth, length, 2 * D_HY), F32), name="hyena_filter",
        compiler_params=pltpu.CompilerParams(vmem_limit_bytes=VMEM_LIMIT),
    )(z2, decay2, *arrs)

    nb = length // HC
    nd = 2 * nb - 1
    blk = lambda off: pl.BlockSpec((1, HC, D_HY), lambda l, d: (l, (d + off) % nb, (d + off) // nb))
    out = pl.BlockSpec((1, 1, HC, D_HY), lambda l, d: (l, d, 0, 0))
    g = jax.ShapeDtypeStruct((depth, nd, HC, D_HY), F32)
    return pl.pallas_call(
        _spectrum_kernel, grid=(depth, nd), in_specs=[_const_spec(table.shape), blk(1), blk(0)],
        out_specs=(out, out), out_shape=(g, g), name="hyena_spectrum",
        compiler_params=pltpu.CompilerParams(vmem_limit_bytes=VMEM_LIMIT),
    )(table, k_lag, k_lag)


def _dft_table():
    f = jnp.arange(HC, dtype=jnp.int32)[:, None]
    r = jnp.arange(HC, dtype=jnp.int32)[None, :]
    ang = (((2 * f + 1) * r) % (4 * HC)).astype(F32) * (math.pi / (2 * HC))
    return jnp.concatenate([jnp.cos(ang), jnp.sin(ang)], axis=0)


def _hyena_kernel(x0_ref, x1_ref, v_ref, cw_ref, cb_ref, skip_ref, gc_ref, gs_ref, f_ref, ft_ref,
                  y_ref, zf_s, zb_s, x0_s, zc_s, zs_s, yy_s):
    length, ch = zf_s.shape
    nb = length // HC
    row = lax.broadcasted_iota(jnp.int32, (length, ch), 0)

    def short_conv(ref, role):
        u = ref[0].astype(F32)
        prev = jnp.where(row == 0, 0.0, pltpu.roll(u, 1, axis=0))
        nxt = jnp.where(row == length - 1, 0.0, pltpu.roll(u, length - 1, axis=0))
        w0, w1, w2 = (cw_ref[3 * tap + role:3 * tap + role + 1, :] for tap in range(3))
        return prev * w0 + u * w1 + nxt * w2 + cb_ref[role:role + 1, :]

    x0_s[...] = short_conv(x0_ref, 0)
    z = short_conv(v_ref, 2) * short_conv(x1_ref, 1)
    zf_s[...] = z
    zb_s[...] = z.astype(BF16)

    for j in range(nb):
        zz = _dot(f_ref[...], zb_s[j * HC:(j + 1) * HC, :])
        zc_s[j] = zz[:HC]
        zs_s[j] = zz[HC:]

    def mix(t, carry):
        r = pl.ds(pl.multiple_of(t * MIX_ROWS, MIX_ROWS), MIX_ROWS)
        r_sin = pl.ds(pl.multiple_of(HC + t * MIX_ROWS, MIX_ROWS), MIX_ROWS)
        zc = [zc_s[j, r, :] for j in range(nb)]
        zs = [zs_s[j, r, :] for j in range(nb)]
        gc = [gc_ref[0, d, r, :] for d in range(2 * nb - 1)]
        gs = [gs_ref[0, d, r, :] for d in range(2 * nb - 1)]
        for i in range(nb):
            yc = ys = None
            for j in range(nb):
                d = i - j + nb - 1
                tc = zc[j] * gc[d] - zs[j] * gs[d]
                ts = zc[j] * gs[d] + zs[j] * gc[d]
                yc = tc if yc is None else yc + tc
                ys = ts if ys is None else ys + ts
            yy_s[i, r, :] = yc.astype(BF16)
            yy_s[i, r_sin, :] = ys.astype(BF16)
        return carry

    lax.fori_loop(0, HC // MIX_ROWS, mix, 0)

    for i in range(nb):
        rows = slice(i * HC, (i + 1) * HC)
        y = _dot(ft_ref[...], yy_s[i])
        y = (y + zf_s[rows, :] * skip_ref[...]) * x0_s[rows, :]
        y_ref[0, rows, :] = y.astype(BF16)


def _hyena_call(hy3, cw, cb, skip, gc, gs, layer, table_b, table_t):
    b, length, _ = hy3.shape
    ch = D_HY // 2
    nh = D_HY // ch
    nb = length // HC

    def role_spec(role):
        return pl.BlockSpec((1, length, ch), lambda j, i: (i, 0, role * nh + j))

    g_blk = pl.BlockSpec((1, 2 * nb - 1, HC, ch), lambda j, i: (layer, 0, 0, j),
                         pipeline_mode=pl.Buffered(1))
    in_specs = [
        role_spec(0), role_spec(1), role_spec(2),
        pl.BlockSpec((9, ch), lambda j, i: (0, j)),
        pl.BlockSpec((3, ch), lambda j, i: (0, j)),
        pl.BlockSpec((1, ch), lambda j, i: (0, j)),
        g_blk, g_blk,
        _const_spec(table_b.shape), _const_spec(table_t.shape),
    ]
    return pl.pallas_call(
        _hyena_kernel, grid=(nh, b), in_specs=in_specs,
        out_specs=pl.BlockSpec((1, length, ch), lambda j, i: (i, 0, j)),
        out_shape=jax.ShapeDtypeStruct((b, length, D_HY), BF16),
        scratch_shapes=[
            pltpu.VMEM((length, ch), F32), pltpu.VMEM((length, ch), BF16),
            pltpu.VMEM((length, ch), F32), pltpu.VMEM((nb, HC, ch), F32),
            pltpu.VMEM((nb, HC, ch), F32), pltpu.VMEM((nb, 2 * HC, ch), BF16),
        ],
        name="hyena",
        compiler_params=pltpu.CompilerParams(
            dimension_semantics=("parallel", "parallel"), vmem_limit_bytes=VMEM_LIMIT),
    )(hy3, hy3, hy3, cw, cb, skip, gc, gs, table_b, table_t)


def _merge_kernel(final_norm, x_ref, ao_ref, hy_ref, gm_ref, wga_ref, wgb_ref, wap_ref, whp_ref,
                  wo_ref, gf_ref, wg_ref, wu_ref, wd_ref, gfin_ref, o_ref):
    x = x_ref[...]
    hb = _rms(x, gm_ref[...]).astype(BF16)
    merged = jax.nn.sigmoid(_dot(hb, wga_ref[...])) * _dot(ao_ref[...], wap_ref[...])
    merged = merged + jax.nn.sigmoid(_dot(hb, wgb_ref[...])) * _dot(hy_ref[...], whp_ref[...])
    x = x + _dot(merged.astype(BF16), wo_ref[...])
    hb = _rms(x, gf_ref[...]).astype(BF16)
    for start, width in FFN_CHUNKS:
        cols = slice(start, start + width)
        act = jax.nn.silu(_dot(hb, wg_ref[:, cols])) * _dot(hb, wu_ref[:, cols])
        x = x + _dot(act.astype(BF16), wd_ref[cols, :])
    if final_norm:
        x = _rms(x, gfin_ref[...])
    o_ref[...] = x


def _merge_call(x2, ao2, hy2, gm, wga, wgb, wap, whp, wo, gf, wg, wu, wd, gfin, final_norm):
    n = x2.shape[0]
    tm = TM_MERGE
    row = lambda i: (i, 0)
    consts = [gm, wga, wgb, wap, whp, wo, gf, wg, wu, wd, gfin]
    in_specs = [
        pl.BlockSpec((tm, D_MODEL), row),
        pl.BlockSpec((tm, ao2.shape[1]), row),
        pl.BlockSpec((tm, hy2.shape[1]), row),
    ] + [_const_spec(c.shape) for c in consts]
    return pl.pallas_call(
        functools.partial(_merge_kernel, final_norm), grid=(n // tm,), in_specs=in_specs,
        out_specs=pl.BlockSpec((tm, D_MODEL), row),
        out_shape=jax.ShapeDtypeStruct((n, D_MODEL), F32), name="merge_ffn",
        compiler_params=pltpu.CompilerParams(
            dimension_semantics=("parallel",), vmem_limit_bytes=VMEM_LIMIT),
    )(x2, ao2, hy2, *consts)


def _swap_pairs(w):
    w2 = w.reshape(w.shape[0], -1, 2)
    return jnp.stack([-w2[..., 1], w2[..., 0]], axis=-1).reshape(w.shape)


def _layer_weights(w_in, w_q_up, w_kv_up):
    c0, c1, c2, c3 = Q_LORA, Q_LORA + KV_LORA, Q_LORA + KV_LORA + QK_ROPE, Q_LORA + KV_LORA + QK_ROPE + 3 * D_HY
    w_kpe = w_in[:, c1:c2]
    pad = jnp.zeros((D_MODEL, P1_W - c2 - QK_ROPE), F32)
    w1 = jnp.concatenate([w_in[:, :c1], w_kpe, _swap_pairs(w_kpe), pad], axis=1).astype(BF16)
    why = w_in[:, c2:c3].astype(BF16)
    wga = w_in[:, c3:c3 + D_MODEL].astype(BF16)
    wgb = w_in[:, c3 + D_MODEL:].astype(BF16)

    wq3 = w_q_up.reshape(Q_LORA, N_HEADS, QK_NOPE + QK_ROPE)
    wq_pe = wq3[..., QK_NOPE:]
    wq = jnp.concatenate([wq3[..., :QK_NOPE], wq_pe, _swap_pairs(wq_pe.reshape(Q_LORA, -1)).reshape(wq_pe.shape)],
                         axis=-1).reshape(Q_LORA, QKV_W).astype(BF16)

    wkv3 = w_kv_up.reshape(KV_LORA, N_HEADS, QK_NOPE + V_HEAD)
    zpad = jnp.zeros((KV_LORA, N_HEADS, HEAD_PAD - QK_NOPE), F32)
    wk = jnp.concatenate([wkv3[..., :QK_NOPE], zpad], axis=-1).reshape(KV_LORA, QKV_W).astype(BF16)
    wv3 = wkv3[..., QK_NOPE:]
    odd = (jnp.arange(N_HEADS) % 2 == 1)[None, :, None]
    wv = jnp.where(odd, jnp.concatenate([zpad, wv3], axis=-1), jnp.concatenate([wv3, zpad], axis=-1))
    wv = wv.reshape(KV_LORA, QKV_W).astype(BF16)
    return w1, why, wga, wgb, wq.T, wk, wv.T


def _static_tables():
    lane = jnp.arange(QKV_W) % HEAD_PAD
    head = jnp.arange(QKV_W) // HEAD_PAD
    vone = jnp.where(head % 2 == 0, lane == V_HEAD, lane == 0).astype(F32)[:, None]
    src = jnp.arange(LANES)[:, None]
    place = ((src < QK_ROPE) & (lane[None, :] >= QK_NOPE) & ((lane[None, :] - QK_NOPE) % QK_ROPE == src))
    return vone, place.astype(BF16)


def kernel(x, positions, mix_norm_g, w_in, q_norm_g, w_q_up, kv_norm_g, w_kv_up, w_attn_proj, hy_conv_w, hy_conv_b, filt_w1, filt_b1, filt_f1, filt_w2, filt_b2, filt_f2, filt_w3, filt_b3, filt_f3, filt_w4, hy_skip, w_hy_proj, w_out, ffn_norm_g, w_gate, w_up, w_down, final_norm_g):
    b, seq, d = x.shape
    depth = w_in.shape[0]
    qtab, kcos, ksin = _rope_tables(positions)
    vone, place = _static_tables()
    table = _dft_table()
    table_b = table.astype(BF16)
    fw = dict(w1=filt_w1, b1=filt_b1, f1=filt_f1, w2=filt_w2, b2=filt_b2, f2=filt_f2,
              w3=filt_w3, b3=filt_b3, f3=filt_f3, w4=filt_w4)
    g_cos, g_sin = _filter_spectrum(seq, table, fw)

    x2 = x.reshape(b * seq, d)
    for i in range(depth):
        w1, why, wga, wgb, wqt, wk, wvt = _layer_weights(w_in[i], w_q_up[i], w_kv_up[i])
        q3, k2, vt, hy2 = _proj_call(
            x2, seq, mix_norm_g[i][None], w1, why, q_norm_g[i][None], wqt, kv_norm_g[i][None],
            wk, wvt, vone, place, qtab, kcos, ksin)
        ao = _attn_call(q3, k2.reshape(b, seq, QKV_W), vt)
        hy = _hyena_call(
            hy2.reshape(b, seq, 3 * D_HY), hy_conv_w[i].reshape(9, D_HY), hy_conv_b[i].reshape(3, D_HY),
            hy_skip[i][None], g_cos, g_sin, i, table_b, table_b.T)
        x2 = _merge_call(
            x2, ao.reshape(b * seq, -1), hy.reshape(b * seq, -1), mix_norm_g[i][None], wga, wgb,
            w_attn_proj[i].astype(BF16), w_hy_proj[i].astype(BF16), w_out[i].astype(BF16),
            ffn_norm_g[i][None], w_gate[i].astype(BF16), w_up[i].astype(BF16), w_down[i].astype(BF16),
            final_norm_g[None], final_norm=(i == depth - 1))
    return x2.reshape(b, seq, d)
```

```python
import functools
import math

import jax
import jax.numpy as jnp
from jax import lax
from jax.experimental import pallas as pl
from jax.experimental.pallas import tpu as pltpu

D_MODEL = 1024
N_HEADS = 8
Q_LORA = 256
KV_LORA = 128
QK_NOPE = 64
QK_ROPE = 32
V_HEAD = 64
ROPE_THETA = 10000.0
D_HY = 512
FILTER_EMB = 33
FILTER_HIDDEN = 64
FAST_DECAY_PCT = 0.3
SLOW_DECAY_PCT = 1.5
DECAY_TARGET = 1e-2
MOD_SHIFT = 0.0
EPS = 1e-6

LANES = 128
HEAD_PAD = LANES
QKV_W = N_HEADS * HEAD_PAD
P1_W = 512
VMEM_LIMIT = 56 * 1024 * 1024

TM_PROJ = 512
TM_MERGE = 512
QT = 512
KC = 1024
SUB = 512
HC = 512
MIX_ROWS = 32
FFN_CHUNKS = ((0, 1536), (1536, 1280))

F32 = jnp.float32
BF16 = jnp.bfloat16


def _const_spec(shape):
    nd = len(shape)
    return pl.BlockSpec(shape, lambda *_: (0,) * nd, pipeline_mode=pl.Buffered(1))


def _rms(x, g):
    return x * lax.rsqrt(jnp.mean(x * x, axis=-1, keepdims=True) + EPS) * g


def _dot(a, b):
    return jnp.dot(a, b, preferred_element_type=F32)


def _rope_kernel(pos_col_ref, pos_row_ref, invq_ref, invk_ref, qtab_ref, kcos_ref, ksin_ref):
    sub = lax.broadcasted_iota(jnp.int32, qtab_ref.shape, 0)
    scale = (QK_NOPE + QK_ROPE) ** -0.5 * math.log2(math.e)
    angq = pos_row_ref[...].astype(F32) * invq_ref[...]
    rot = jnp.where(sub < QK_NOPE + QK_ROPE, jnp.cos(angq), jnp.sin(angq))
    qtab_ref[...] = jnp.where(sub < QK_NOPE, 1.0, rot) * scale
    lane = lax.broadcasted_iota(jnp.int32, kcos_ref.shape, 1)
    angk = pos_col_ref[...].astype(F32) * invk_ref[...]
    kcos_ref[...] = jnp.where(lane < QK_ROPE, jnp.cos(angk), 0.0)
    ksin_ref[...] = jnp.where(lane < QK_ROPE, jnp.sin(angk), 0.0)


def _rope_tables(positions):
    s = positions.shape[0]
    inv = 1.0 / (ROPE_THETA ** (jnp.arange(0, QK_ROPE, 2, dtype=F32) / QK_ROPE))
    inv_pairs = jnp.repeat(inv, 2)
    zeros = jnp.zeros((QK_NOPE,), F32)
    invq = jnp.concatenate([zeros, inv_pairs, inv_pairs])[:, None]
    invk = jnp.concatenate([inv_pairs, jnp.zeros((LANES - QK_ROPE,), F32)])[None, :]
    return pl.pallas_call(
        _rope_kernel, name="rope_tables",
        out_shape=(jax.ShapeDtypeStruct((LANES, s), F32), jax.ShapeDtypeStruct((s, LANES), F32),
                   jax.ShapeDtypeStruct((s, LANES), F32)),
    )(positions.reshape(s, 1), positions.reshape(1, s), invq, invk)


_NT = (((1,), (1,)), ((), ()))


def _store_tiles(out_ref, val):
    width = out_ref.shape[2]
    for c in range(out_ref.shape[0]):
        out_ref[c] = val[:, c * width:(c + 1) * width]


def _tile_spec(tile, tm):
    if tile >= tm:
        per = tile // tm
        return pl.BlockSpec((1, QKV_W, tm), lambda i: (i // per, 0, i % per))
    return pl.BlockSpec((tm // tile, QKV_W, tile), lambda i: (i, 0, 0))


def _proj_kernel(x_ref, g_ref, w1_ref, why_ref, gq_ref, wqt_ref, gkv_ref, wk_ref, wvt_ref,
                 vone_ref, place_ref, qtab_ref, kcos_ref, ksin_ref,
                 q_out, k_out, v_out, hy_out):
    hb = _rms(x_ref[...], g_ref[...]).astype(BF16)
    p1 = _dot(hb, w1_ref[...])
    hy_out[...] = _dot(hb, why_ref[...]).astype(BF16)

    qn = _rms(p1[:, :Q_LORA], gq_ref[...]).astype(BF16)
    qtab = jnp.concatenate([qtab_ref[...]] * N_HEADS, axis=0)
    q_t = (lax.dot_general(wqt_ref[...], qn, _NT, preferred_element_type=F32) * qtab).astype(BF16)
    _store_tiles(q_out, q_t)

    kvn = _rms(p1[:, Q_LORA:Q_LORA + KV_LORA], gkv_ref[...]).astype(BF16)
    t = p1[:, Q_LORA + KV_LORA:]
    k_rot = t * kcos_ref[...] + pltpu.roll(t, LANES - QK_ROPE, axis=1) * ksin_ref[...]
    k_out[...] = (_dot(kvn, wk_ref[...]) + _dot(k_rot.astype(BF16), place_ref[...])).astype(BF16)
    v_t = lax.dot_general(wvt_ref[...], kvn, _NT, preferred_element_type=F32) + vone_ref[...]
    _store_tiles(v_out, v_t.astype(BF16))


def _proj_call(x2, seq, g, w1, why, gq, wqt, gkv, wk, wvt, vone, place, qtab, kcos, ksin):
    n = x2.shape[0]
    tm = TM_PROJ
    tiles_per_seq = seq // tm
    row = lambda i: (i, 0)
    tab = lambda i: (i % tiles_per_seq, 0)
    in_specs = [
        pl.BlockSpec((tm, D_MODEL), row),
        _const_spec(g.shape), _const_spec(w1.shape), _const_spec(why.shape),
        _const_spec(gq.shape), _const_spec(wqt.shape), _const_spec(gkv.shape),
        _const_spec(wk.shape), _const_spec(wvt.shape), _const_spec(vone.shape),
        _const_spec(place.shape),
        pl.BlockSpec((LANES, tm), lambda i: (0, i % tiles_per_seq)),
        pl.BlockSpec((tm, LANES), tab), pl.BlockSpec((tm, LANES), tab),
    ]
    out_shape = (
        jax.ShapeDtypeStruct((n // QT, QKV_W, QT), BF16),
        jax.ShapeDtypeStruct((n, QKV_W), BF16),
        jax.ShapeDtypeStruct((n // KC, QKV_W, KC), BF16),
        jax.ShapeDtypeStruct((n, 3 * D_HY), BF16),
    )
    out_specs = (
        _tile_spec(QT, tm), pl.BlockSpec((tm, QKV_W), row),
        _tile_spec(KC, tm), pl.BlockSpec((tm, 3 * D_HY), row),
    )
    return pl.pallas_call(
        _proj_kernel, grid=(n // tm,), in_specs=in_specs, out_specs=out_specs,
        out_shape=out_shape, name="proj",
        compiler_params=pltpu.CompilerParams(
            dimension_semantics=("parallel",), vmem_limit_bytes=VMEM_LIMIT),
    )(x2, g, w1, why, gq, wqt, gkv, wk, wvt, vone, place, qtab, kcos, ksin)


def _attn_kernel(q_ref, k_ref, v_ref, o_ref, s_ref, acc_ref):
    n_tiles, n_chunks = q_ref.shape[0], v_ref.shape[0]
    sub = lax.broadcasted_iota(jnp.int32, (HEAD_PAD, QT), 0)
    units = [(pair, t) for pair in range(N_HEADS // 2) for t in range(n_tiles)]

    def feat(pair, hh):
        return slice((2 * pair + hh) * HEAD_PAD, (2 * pair + hh + 1) * HEAD_PAD)

    def stage_a(kc, sub_i, hh, u, m_run):
        pair, t = units[u]
        keys = pl.ds(pl.multiple_of(kc * KC + sub_i * SUB, SUB), SUB)
        sc = _dot(k_ref[0, keys, feat(pair, hh)], q_ref[t, feat(pair, hh), :])
        s_ref[u % 2, hh, keys, :] = sc
        return jnp.maximum(m_run, jnp.max(sc.reshape(SUB // 8, 8, QT), axis=0))

    def stage_b(kc, sub_i, hh, u, m):
        pair, _ = units[u]
        keys = pl.ds(pl.multiple_of(kc * KC + sub_i * SUB, SUB), SUB)
        p = jnp.exp2((s_ref[u % 2, hh, keys, :] - m).astype(BF16))
        acc_ref[hh] += _dot(v_ref[kc, feat(pair, hh), sub_i * SUB:(sub_i + 1) * SUB], p)

    def chunk(kc, u_a, u_b, m_run, m):
        m_run = list(m_run)
        for sub_i in range(KC // SUB):
            for hh in range(2):
                if u_a is not None:
                    m_run[hh] = stage_a(kc, sub_i, hh, u_a, m_run[hh])
                if u_b is not None:
                    stage_b(kc, sub_i, hh, u_b, m[hh])
        return tuple(m_run)

    neg = (jnp.full((8, QT), -jnp.inf, F32),) * 2
    m_run = lax.fori_loop(0, n_chunks, lambda kc, mr: chunk(kc, 0, None, mr, None), neg)
    for u, (pair, t) in enumerate(units):
        m = tuple(jnp.max(mr, axis=0, keepdims=True) for mr in m_run)
        acc_ref[...] = jnp.zeros_like(acc_ref)
        if u + 1 < len(units):
            def both(kc, mr, u=u, m=m):
                return chunk(kc, u + 1, u, mr, m)
            m_run = lax.fori_loop(0, n_chunks, both, neg)
        else:
            def last(kc, carry, u=u, m=m):
                chunk(kc, None, u, (), m)
                return carry
            lax.fori_loop(0, n_chunks, last, 0)
        acc0, acc1 = acc_ref[0], acc_ref[1]
        o_t = jnp.where(sub < V_HEAD, acc0 / acc0[V_HEAD:V_HEAD + 1, :], acc1 / acc1[0:1, :])
        o_ref[0, t * QT:(t + 1) * QT, pair * 2 * V_HEAD:(pair + 1) * 2 * V_HEAD] = o_t.T.astype(BF16)


def _attn_call(q3, k3, vt):
    b, seq, _ = k3.shape
    return pl.pallas_call(
        _attn_kernel, grid=(b,),
        in_specs=[
            pl.BlockSpec((seq // QT, QKV_W, QT), lambda i: (i, 0, 0)),
            pl.BlockSpec((1, seq, QKV_W), lambda i: (i, 0, 0)),
            pl.BlockSpec((seq // KC, QKV_W, KC), lambda i: (i, 0, 0)),
        ],
        out_specs=pl.BlockSpec((1, seq, N_HEADS * V_HEAD), lambda i: (i, 0, 0)),
        out_shape=jax.ShapeDtypeStruct((b, seq, N_HEADS * V_HEAD), BF16),
        scratch_shapes=[pltpu.VMEM((2, 2, seq, QT), F32), pltpu.VMEM((2, HEAD_PAD, QT), F32)],
        name="attn",
        compiler_params=pltpu.CompilerParams(
            dimension_semantics=("parallel",), vmem_limit_bytes=VMEM_LIMIT),
    )(q3, k3, vt)


def _hi_lo(a):
    hi = a.astype(BF16)
    return hi, (a - hi.astype(F32)).astype(BF16)


def _dot3(a, b):
    ah, al = _hi_lo(a)
    bh, bl = _hi_lo(b)
    return _dot(ah, bh) + (_dot(ah, bl) + _dot(al, bh))


def _filter_kernel(z_ref, decay_ref, w1, b1, f1, w2, b2, f2, w3, b3, f3, w4, filt_ref):
    h = jnp.sin(f1[0] * (_dot3(z_ref[...], w1[0]) + b1[0]))
    h = jnp.sin(f2[0] * (_dot3(h, w2[0]) + b2[0]))
    h = jnp.sin(f3[0] * (_dot3(h, w3[0]) + b3[0]))
    filt_ref[0] = _dot3(h, w4[0]) * decay_ref[...]


def _spectrum_kernel(f_ref, kd_ref, km_ref, gc_ref, gs_ref):
    kd = _dot3(f_ref[...], kd_ref[0])
    km = _dot3(f_ref[...], km_ref[0])
    row = lax.broadcasted_iota(jnp.int32, (HC, D_HY), 0)
    sigma = jnp.where(row % 2 == 0, 1.0, -1.0)
    gc_ref[0, 0] = (kd[:HC] + sigma * km[HC:]) * (1.0 / HC)
    gs_ref[0, 0] = (kd[HC:] - sigma * km[:HC]) * (1.0 / HC)


def _filter_spectrum(length, table, fw):
    depth = fw["w1"].shape[0]
    t = jnp.linspace(0.0, 1.0, length, dtype=F32)[:, None]
    bands = (FILTER_EMB - 1) // 2
    freqs = jnp.linspace(1e-4, bands - 1, bands, dtype=F32)[None, :]
    w = 2.0 * math.pi * jnp.arange(length, dtype=F32)[:, None] / length
    deltas = jnp.abs(jnp.linspace(math.log(DECAY_TARGET) / FAST_DECAY_PCT,
                                  math.log(DECAY_TARGET) / SLOW_DECAY_PCT, D_HY, dtype=F32))[None, :]

    half = LANES // 2
    m_idx = jnp.arange(length)
    pos = (length - m_idx) % length

    def features(tt, ww):
        zf = jnp.concatenate([tt, jnp.cos(freqs * ww), -jnp.sin(freqs * ww)], axis=-1)
        return jnp.pad(zf, ((0, 0), (0, half - FILTER_EMB)))

    z2 = jnp.concatenate([features(t[pos], w[pos]), features(t, w)], axis=1)
    decay_bwd = jnp.where(m_idx[:, None] == 0, 0.0, jnp.exp(-t[pos] * deltas) + MOD_SHIFT)
    decay2 = jnp.concatenate([decay_bwd, jnp.exp(-t * deltas) + MOD_SHIFT], axis=1)

    def lay(name):
        a = fw[name]
        if a.ndim == 2:
            a = jnp.pad(a, ((0, 0), (0, half - a.shape[1])))
            a = jnp.concatenate([a, a], axis=1)[:, None, :]
        elif name == "w4":
            zero = jnp.zeros_like(a[..., :D_HY])
            a = jnp.concatenate([jnp.concatenate([a[..., D_HY:], zero], axis=2),
                                 jnp.concatenate([zero, a[..., :D_HY]], axis=2)], axis=1)
        else:
            a = jnp.pad(a, ((0, 0), (0, half - a.shape[1]), (0, half - a.shape[2])))
            zero = jnp.zeros_like(a)
            a = jnp.concatenate([jnp.concatenate([a, zero], axis=2), jnp.concatenate([zero, a], axis=2)], axis=1)
        return a, pl.BlockSpec((1,) + a.shape[1:], lambda l: (l, 0, 0))

    names = ["w1", "b1", "f1", "w2", "b2", "f2", "w3", "b3", "f3", "w4"]
    arrs, specs = zip(*[lay(nm) for nm in names])
    k_lag = pl.pallas_call(
        _filter_kernel, grid=(depth,),
        in_specs=[_const_spec(z2.shape), _const_spec(decay2.shape), *specs],
        out_specs=pl.BlockSpec((1, length, 2 * D_HY), lambda l: (l, 0, 0)),
        out_shape=jax.ShapeDtypeStruct((depth, length, 2 * D_HY), F32), name="hyena_filter",
        compiler_params=pltpu.CompilerParams(vmem_limit_bytes=VMEM_LIMIT),
    )(z2, decay2, *arrs)

    nb = length // HC
    nd = 2 * nb - 1
    blk = lambda off: pl.BlockSpec((1, HC, D_HY), lambda l, d: (l, (d + off) % nb, (d + off) // nb))
    out = pl.BlockSpec((1, 1, HC, D_HY), lambda l, d: (l, d, 0, 0))
    g = jax.ShapeDtypeStruct((depth, nd, HC, D_HY), F32)
    return pl.pallas_call(
        _spectrum_kernel, grid=(depth, nd), in_specs=[_const_spec(table.shape), blk(1), blk(0)],
        out_specs=(out, out), out_shape=(g, g), name="hyena_spectrum",
        compiler_params=pltpu.CompilerParams(vmem_limit_bytes=VMEM_LIMIT),
    )(table, k_lag, k_lag)


def _dft_table():
    f = jnp.arange(HC, dtype=jnp.int32)[:, None]
    r = jnp.arange(HC, dtype=jnp.int32)[None, :]
    ang = (((2 * f + 1) * r) % (4 * HC)).astype(F32) * (math.pi / (2 * HC))
    return jnp.concatenate([jnp.cos(ang), jnp.sin(ang)], axis=0)


def _hyena_kernel(x0_ref, x1_ref, v_ref, cw_ref, cb_ref, skip_ref, gc_ref, gs_ref, f_ref, ft_ref,
                  y_ref, zf_s, zb_s, x0_s, zc_s, zs_s, yy_s):
    length, ch = zf_s.shape
    nb = length // HC
    row = lax.broadcasted_iota(jnp.int32, (length, ch), 0)

    def short_conv(ref, role):
        u = ref[0].astype(F32)
        prev = jnp.where(row == 0, 0.0, pltpu.roll(u, 1, axis=0))
        nxt = jnp.where(row == length - 1, 0.0, pltpu.roll(u, length - 1, axis=0))
        w0, w1, w2 = (cw_ref[3 * tap + role:3 * tap + role + 1, :] for tap in range(3))
        return prev * w0 + u * w1 + nxt * w2 + cb_ref[role:role + 1, :]

    x0_s[...] = short_conv(x0_ref, 0)
    z = short_conv(v_ref, 2) * short_conv(x1_ref, 1)
    zf_s[...] = z
    zb_s[...] = z.astype(BF16)

    for j in range(nb):
        zz = _dot(f_ref[...], zb_s[j * HC:(j + 1) * HC, :])
        zc_s[j] = zz[:HC]
        zs_s[j] = zz[HC:]

    def mix(t, carry):
        r = pl.ds(pl.multiple_of(t * MIX_ROWS, MIX_ROWS), MIX_ROWS)
        r_sin = pl.ds(pl.multiple_of(HC + t * MIX_ROWS, MIX_ROWS), MIX_ROWS)
        zc = [zc_s[j, r, :] for j in range(nb)]
        zs = [zs_s[j, r, :] for j in range(nb)]
        gc = [gc_ref[0, d, r, :] for d in range(2 * nb - 1)]
        gs = [gs_ref[0, d, r, :] for d in range(2 * nb - 1)]
        for i in range(nb):
            yc = ys = None
            for j in range(nb):
                d = i - j + nb - 1
                tc = zc[j] * gc[d] - zs[j] * gs[d]
                ts = zc[j] * gs[d] + zs[j] * gc[d]
                yc = tc if yc is None else yc + tc
                ys = ts if ys is None else ys + ts
            yy_s[i, r, :] = yc.astype(BF16)
            yy_s[i, r_sin, :] = ys.astype(BF16)
        return carry

    lax.fori_loop(0, HC // MIX_ROWS, mix, 0)

    for i in range(nb):
        rows = slice(i * HC, (i + 1) * HC)
        y = _dot(ft_ref[...], yy_s[i])
        y = (y + zf_s[rows, :] * skip_ref[...]) * x0_s[rows, :]
        y_ref[0, rows, :] = y.astype(BF16)


def _hyena_call(hy3, cw, cb, skip, gc, gs, layer, table_b, table_t):
    b, length, _ = hy3.shape
    ch = D_HY // 2
    nh = D_HY // ch
    nb = length // HC

    def role_spec(role):
        return pl.BlockSpec((1, length, ch), lambda j, i: (i, 0, role * nh + j))

    g_blk = pl.BlockSpec((1, 2 * nb - 1, HC, ch), lambda j, i: (layer, 0, 0, j),
                         pipeline_mode=pl.Buffered(1))
    in_specs = [
        role_spec(0), role_spec(1), role_spec(2),
        pl.BlockSpec((9, ch), lambda j, i: (0, j)),
        pl.BlockSpec((3, ch), lambda j, i: (0, j)),
        pl.BlockSpec((1, ch), lambda j, i: (0, j)),
        g_blk, g_blk,
        _const_spec(table_b.shape), _const_spec(table_t.shape),
    ]
    return pl.pallas_call(
        _hyena_kernel, grid=(nh, b), in_specs=in_specs,
        out_specs=pl.BlockSpec((1, length, ch), lambda j, i: (i, 0, j)),
        out_shape=jax.ShapeDtypeStruct((b, length, D_HY), BF16),
        scratch_shapes=[
            pltpu.VMEM((length, ch), F32), pltpu.VMEM((length, ch), BF16),
            pltpu.VMEM((length, ch), F32), pltpu.VMEM((nb, HC, ch), F32),
            pltpu.VMEM((nb, HC, ch), F32), pltpu.VMEM((nb, 2 * HC, ch), BF16),
        ],
        name="hyena",
        compiler_params=pltpu.CompilerParams(
            dimension_semantics=("parallel", "parallel"), vmem_limit_bytes=VMEM_LIMIT),
    )(hy3, hy3, hy3, cw, cb, skip, gc, gs, table_b, table_t)


def _merge_kernel(final_norm, x_ref, ao_ref, hy_ref, gm_ref, wga_ref, wgb_ref, wap_ref, whp_ref,
                  wo_ref, gf_ref, wg_ref, wu_ref, wd_ref, gfin_ref, o_ref):
    x = x_ref[...]
    hb = _rms(x, gm_ref[...]).astype(BF16)
    merged = jax.nn.sigmoid(_dot(hb, wga_ref[...])) * _dot(ao_ref[...], wap_ref[...])
    merged = merged + jax.nn.sigmoid(_dot(hb, wgb_ref[...])) * _dot(hy_ref[...], whp_ref[...])
    x = x + _dot(merged.astype(BF16), wo_ref[...])
    hb = _rms(x, gf_ref[...]).astype(BF16)
    for start, width in FFN_CHUNKS:
        cols = slice(start, start + width)
        act = jax.nn.silu(_dot(hb, wg_ref[:, cols])) * _dot(hb, wu_ref[:, cols])
        x = x + _dot(act.astype(BF16), wd_ref[cols, :])
    if final_norm:
        x = _rms(x, gfin_ref[...])
    o_ref[...] = x


def _merge_call(x2, ao2, hy2, gm, wga, wgb, wap, whp, wo, gf, wg, wu, wd, gfin, final_norm):
    n = x2.shape[0]
    tm = TM_MERGE
    row = lambda i: (i, 0)
    consts = [gm, wga, wgb, wap, whp, wo, gf, wg, wu, wd, gfin]
    in_specs = [
        pl.BlockSpec((tm, D_MODEL), row),
        pl.BlockSpec((tm, ao2.shape[1]), row),
        pl.BlockSpec((tm, hy2.shape[1]), row),
    ] + [_const_spec(c.shape) for c in consts]
    return pl.pallas_call(
        functools.partial(_merge_kernel, final_norm), grid=(n // tm,), in_specs=in_specs,
        out_specs=pl.BlockSpec((tm, D_MODEL), row),
        out_shape=jax.ShapeDtypeStruct((n, D_MODEL), F32), name="merge_ffn",
        compiler_params=pltpu.CompilerParams(
            dimension_semantics=("parallel",), vmem_limit_bytes=VMEM_LIMIT),
    )(x2, ao2, hy2, *consts)


def _swap_pairs(w):
    w2 = w.reshape(w.shape[0], -1, 2)
    return jnp.stack([-w2[..., 1], w2[..., 0]], axis=-1).reshape(w.shape)


def _layer_weights(w_in, w_q_up, w_kv_up):
    c0, c1, c2, c3 = Q_LORA, Q_LORA + KV_LORA, Q_LORA + KV_LORA + QK_ROPE, Q_LORA + KV_LORA + QK_ROPE + 3 * D_HY
    w_kpe = w_in[:, c1:c2]
    pad = jnp.zeros((D_MODEL, P1_W - c2 - QK_ROPE), F32)
    w1 = jnp.concatenate([w_in[:, :c1], w_kpe, _swap_pairs(w_kpe), pad], axis=1).astype(BF16)
    why = w_in[:, c2:c3].astype(BF16)
    wga = w_in[:, c3:c3 + D_MODEL].astype(BF16)
    wgb = w_in[:, c3 + D_MODEL:].astype(BF16)

    wq3 = w_q_up.reshape(Q_LORA, N_HEADS, QK_NOPE + QK_ROPE)
    wq_pe = wq3[..., QK_NOPE:]
    wq = jnp.concatenate([wq3[..., :QK_NOPE], wq_pe, _swap_pairs(wq_pe.reshape(Q_LORA, -1)).reshape(wq_pe.shape)],
                         axis=-1).reshape(Q_LORA, QKV_W).astype(BF16)

    wkv3 = w_kv_up.reshape(KV_LORA, N_HEADS, QK_NOPE + V_HEAD)
    zpad = jnp.zeros((KV_LORA, N_HEADS, HEAD_PAD - QK_NOPE), F32)
    wk = jnp.concatenate([wkv3[..., :QK_NOPE], zpad], axis=-1).reshape(KV_LORA, QKV_W).astype(BF16)
    wv3 = wkv3[..., QK_NOPE:]
    odd = (jnp.arange(N_HEADS) % 2 == 1)[None, :, None]
    wv = jnp.where(odd, jnp.concatenate([zpad, wv3], axis=-1), jnp.concatenate([wv3, zpad], axis=-1))
    wv = wv.reshape(KV_LORA, QKV_W).astype(BF16)
    return w1, why, wga, wgb, wq.T, wk, wv.T


def _static_tables():
    lane = jnp.arange(QKV_W) % HEAD_PAD
    head = jnp.arange(QKV_W) // HEAD_PAD
    vone = jnp.where(head % 2 == 0, lane == V_HEAD, lane == 0).astype(F32)[:, None]
    src = jnp.arange(LANES)[:, None]
    place = ((src < QK_ROPE) & (lane[None, :] >= QK_NOPE) & ((lane[None, :] - QK_NOPE) % QK_ROPE == src))
    return vone, place.astype(BF16)


def kernel(x, positions, mix_norm_g, w_in, q_norm_g, w_q_up, kv_norm_g, w_kv_up, w_attn_proj, hy_conv_w, hy_conv_b, filt_w1, filt_b1, filt_f1, filt_w2, filt_b2, filt_f2, filt_w3, filt_b3, filt_f3, filt_w4, hy_skip, w_hy_proj, w_out, ffn_norm_g, w_gate, w_up, w_down, final_norm_g):
    b, seq, d = x.shape
    depth = w_in.shape[0]
    qtab, kcos, ksin = _rope_tables(positions)
    vone, place = _static_tables()
    table = _dft_table()
    table_b = table.astype(BF16)
    fw = dict(w1=filt_w1, b1=filt_b1, f1=filt_f1, w2=filt_w2, b2=filt_b2, f2=filt_f2,
              w3=filt_w3, b3=filt_b3, f3=filt_f3, w4=filt_w4)
    g_cos, g_sin = _filter_spectrum(seq, table, fw)

    x2 = x.reshape(b * seq, d)
    for i in range(depth):
        w1, why, wga, wgb, wqt, wk, wvt = _layer_weights(w_in[i], w_q_up[i], w_kv_up[i])
        q3, k2, vt, hy2 = _proj_call(
            x2, seq, mix_norm_g[i][None], w1, why, q_norm_g[i][None], wqt, kv_norm_g[i][None],
            wk, wvt, vone, place, qtab, kcos, ksin)
        ao = _attn_call(q3, k2.reshape(b, seq, QKV_W), vt)
        hy = _hyena_call(
            hy2.reshape(b, seq, 3 * D_HY), hy_conv_w[i].reshape(9, D_HY), hy_conv_b[i].reshape(3, D_HY),
            hy_skip[i][None], g_cos, g_sin, i, table_b, table_b.T)
        x2 = _merge_call(
            x2, ao.reshape(b * seq, -1), hy.reshape(b * seq, -1), mix_norm_g[i][None], wga, wgb,
            w_attn_proj[i].astype(BF16), w_hy_proj[i].astype(BF16), w_out[i].astype(BF16),
            ffn_norm_g[i][None], w_gate[i].astype(BF16), w_up[i].astype(BF16), w_down[i].astype(BF16),
            final_norm_g[None], final_norm=(i == depth - 1))
    return x2.reshape(b, seq, d)
```

```python
import functools
import math

import jax
import jax.numpy as jnp
from jax import lax
from jax.experimental import pallas as pl
from jax.experimental.pallas import tpu as pltpu

D_MODEL = 1024
N_HEADS = 8
Q_LORA = 256
KV_LORA = 128
QK_NOPE = 64
QK_ROPE = 32
V_HEAD = 64
ROPE_THETA = 10000.0
D_HY = 512
FILTER_EMB = 33
FILTER_HIDDEN = 64
FAST_DECAY_PCT = 0.3
SLOW_DECAY_PCT = 1.5
DECAY_TARGET = 1e-2
MOD_SHIFT = 0.0
EPS = 1e-6

LANES = 128
HEAD_PAD = LANES
QKV_W = N_HEADS * HEAD_PAD
P1_W = 512
VMEM_LIMIT = 56 * 1024 * 1024

TM_PROJ = 512
TM_MERGE = 512
QT = 512
KC = 1024
SUB = 512
HC = 512
CONV_ROWS = 128
CONV_HALO = 16
MIX_ROWS = 32
FFN_CHUNKS = ((0, 1536), (1536, 1280))

F32 = jnp.float32
BF16 = jnp.bfloat16


def _const_spec(shape):
    nd = len(shape)
    return pl.BlockSpec(shape, lambda *_: (0,) * nd, pipeline_mode=pl.Buffered(1))


def _rms(x, g):
    return x * lax.rsqrt(jnp.mean(x * x, axis=-1, keepdims=True) + EPS) * g


def _dot(a, b):
    return jnp.dot(a, b, preferred_element_type=F32)


def _rope_kernel(pos_col_ref, pos_row_ref, invq_ref, invk_ref, qtab_ref, kcos_ref, ksin_ref):
    sub = lax.broadcasted_iota(jnp.int32, qtab_ref.shape, 0)
    scale = (QK_NOPE + QK_ROPE) ** -0.5 * math.log2(math.e)
    angq = pos_row_ref[...].astype(F32) * invq_ref[...]
    rot = jnp.where(sub < QK_NOPE + QK_ROPE, jnp.cos(angq), jnp.sin(angq))
    qtab_ref[...] = jnp.where(sub < QK_NOPE, 1.0, rot) * scale
    lane = lax.broadcasted_iota(jnp.int32, kcos_ref.shape, 1)
    angk = pos_col_ref[...].astype(F32) * invk_ref[...]
    kcos_ref[...] = jnp.where(lane < QK_ROPE, jnp.cos(angk), 0.0)
    ksin_ref[...] = jnp.where(lane < QK_ROPE, jnp.sin(angk), 0.0)


def _rope_tables(positions):
    s = positions.shape[0]
    inv = 1.0 / (ROPE_THETA ** (jnp.arange(0, QK_ROPE, 2, dtype=F32) / QK_ROPE))
    inv_pairs = jnp.repeat(inv, 2)
    zeros = jnp.zeros((QK_NOPE,), F32)
    invq = jnp.concatenate([zeros, inv_pairs, inv_pairs])[:, None]
    invk = jnp.concatenate([inv_pairs, jnp.zeros((LANES - QK_ROPE,), F32)])[None, :]
    return pl.pallas_call(
        _rope_kernel, name="rope_tables",
        out_shape=(jax.ShapeDtypeStruct((LANES, s), F32), jax.ShapeDtypeStruct((s, LANES), F32),
                   jax.ShapeDtypeStruct((s, LANES), F32)),
    )(positions.reshape(s, 1), positions.reshape(1, s), invq, invk)


_NT = (((1,), (1,)), ((), ()))


def _store_tiles(out_ref, val):
    width = out_ref.shape[2]
    for c in range(out_ref.shape[0]):
        out_ref[c] = val[:, c * width:(c + 1) * width]


def _tile_spec(tile, tm):
    if tile >= tm:
        per = tile // tm
        return pl.BlockSpec((1, QKV_W, tm), lambda i: (i // per, 0, i % per))
    return pl.BlockSpec((tm // tile, QKV_W, tile), lambda i: (i, 0, 0))


def _proj_kernel(x_ref, g_ref, w1_ref, why_ref, gq_ref, wqt_ref, gkv_ref, wk_ref, wvt_ref,
                 vone_ref, place_ref, qtab_ref, kcos_ref, ksin_ref,
                 q_out, k_out, v_out, hy_out):
    hb = _rms(x_ref[...], g_ref[...]).astype(BF16)
    p1 = _dot(hb, w1_ref[...])
    hy_out[...] = _dot(hb, why_ref[...]).astype(BF16)

    qn = _rms(p1[:, :Q_LORA], gq_ref[...]).astype(BF16)
    qtab = jnp.concatenate([qtab_ref[...]] * N_HEADS, axis=0)
    q_t = (lax.dot_general(wqt_ref[...], qn, _NT, preferred_element_type=F32) * qtab).astype(BF16)
    _store_tiles(q_out, q_t)

    kvn = _rms(p1[:, Q_LORA:Q_LORA + KV_LORA], gkv_ref[...]).astype(BF16)
    t = p1[:, Q_LORA + KV_LORA:]
    k_rot = t * kcos_ref[...] + pltpu.roll(t, LANES - QK_ROPE, axis=1) * ksin_ref[...]
    k_out[...] = (_dot(kvn, wk_ref[...]) + _dot(k_rot.astype(BF16), place_ref[...])).astype(BF16)
    v_t = lax.dot_general(wvt_ref[...], kvn, _NT, preferred_element_type=F32) + vone_ref[...]
    _store_tiles(v_out, v_t.astype(BF16))


def _proj_call(x2, seq, g, w1, why, gq, wqt, gkv, wk, wvt, vone, place, qtab, kcos, ksin):
    n = x2.shape[0]
    tm = TM_PROJ
    tiles_per_seq = seq // tm
    row = lambda i: (i, 0)
    tab = lambda i: (i % tiles_per_seq, 0)
    in_specs = [
        pl.BlockSpec((tm, D_MODEL), row),
        _const_spec(g.shape), _const_spec(w1.shape), _const_spec(why.shape),
        _const_spec(gq.shape), _const_spec(wqt.shape), _const_spec(gkv.shape),
        _const_spec(wk.shape), _const_spec(wvt.shape), _const_spec(vone.shape),
        _const_spec(place.shape),
        pl.BlockSpec((LANES, tm), lambda i: (0, i % tiles_per_seq)),
        pl.BlockSpec((tm, LANES), tab), pl.BlockSpec((tm, LANES), tab),
    ]
    out_shape = (
        jax.ShapeDtypeStruct((n // QT, QKV_W, QT), BF16),
        jax.ShapeDtypeStruct((n, QKV_W), BF16),
        jax.ShapeDtypeStruct((n // KC, QKV_W, KC), BF16),
        jax.ShapeDtypeStruct((n, 3 * D_HY), BF16),
    )
    out_specs = (
        _tile_spec(QT, tm), pl.BlockSpec((tm, QKV_W), row),
        _tile_spec(KC, tm), pl.BlockSpec((tm, 3 * D_HY), row),
    )
    return pl.pallas_call(
        _proj_kernel, grid=(n // tm,), in_specs=in_specs, out_specs=out_specs,
        out_shape=out_shape, name="proj",
        compiler_params=pltpu.CompilerParams(
            dimension_semantics=("parallel",), vmem_limit_bytes=VMEM_LIMIT),
    )(x2, g, w1, why, gq, wqt, gkv, wk, wvt, vone, place, qtab, kcos, ksin)


def _attn_kernel(q_ref, k_ref, v_ref, o_ref, s_ref, acc_ref):
    n_tiles, n_chunks = q_ref.shape[0], v_ref.shape[0]
    sub = lax.broadcasted_iota(jnp.int32, (HEAD_PAD, QT), 0)
    units = [(pair, t) for pair in range(N_HEADS // 2) for t in range(n_tiles)]

    def feat(pair, hh):
        return slice((2 * pair + hh) * HEAD_PAD, (2 * pair + hh + 1) * HEAD_PAD)

    def stage_a(kc, sub_i, hh, u, m_run):
        pair, t = units[u]
        keys = pl.ds(pl.multiple_of(kc * KC + sub_i * SUB, SUB), SUB)
        sc = _dot(k_ref[0, keys, feat(pair, hh)], q_ref[t, feat(pair, hh), :])
        s_ref[u % 2, hh, keys, :] = sc
        return jnp.maximum(m_run, jnp.max(sc.reshape(SUB // 8, 8, QT), axis=0))

    def stage_b(kc, sub_i, hh, u, m):
        pair, _ = units[u]
        keys = pl.ds(pl.multiple_of(kc * KC + sub_i * SUB, SUB), SUB)
        p = jnp.exp2((s_ref[u % 2, hh, keys, :] - m).astype(BF16))
        acc_ref[hh] += _dot(v_ref[kc, feat(pair, hh), sub_i * SUB:(sub_i + 1) * SUB], p)

    def chunk(kc, u_a, u_b, m_run, m):
        m_run = list(m_run)
        for sub_i in range(KC // SUB):
            for hh in range(2):
                if u_a is not None:
                    m_run[hh] = stage_a(kc, sub_i, hh, u_a, m_run[hh])
                if u_b is not None:
                    stage_b(kc, sub_i, hh, u_b, m[hh])
        return tuple(m_run)

    neg = (jnp.full((8, QT), -jnp.inf, F32),) * 2
    m_run = lax.fori_loop(0, n_chunks, lambda kc, mr: chunk(kc, 0, None, mr, None), neg)
    for u, (pair, t) in enumerate(units):
        m = tuple(jnp.max(mr, axis=0, keepdims=True) for mr in m_run)
        acc_ref[...] = jnp.zeros_like(acc_ref)
        if u + 1 < len(units):
            def both(kc, mr, u=u, m=m):
                return chunk(kc, u + 1, u, mr, m)
            m_run = lax.fori_loop(0, n_chunks, both, neg)
        else:
            def last(kc, carry, u=u, m=m):
                chunk(kc, None, u, (), m)
                return carry
            lax.fori_loop(0, n_chunks, last, 0)
        acc0, acc1 = acc_ref[0], acc_ref[1]
        o_t = jnp.where(sub < V_HEAD, acc0 / acc0[V_HEAD:V_HEAD + 1, :], acc1 / acc1[0:1, :])
        o_ref[0, t * QT:(t + 1) * QT, pair * 2 * V_HEAD:(pair + 1) * 2 * V_HEAD] = o_t.T.astype(BF16)


def _attn_call(q3, k3, vt):
    b, seq, _ = k3.shape
    return pl.pallas_call(
        _attn_kernel, grid=(b,),
        in_specs=[
            pl.BlockSpec((seq // QT, QKV_W, QT), lambda i: (i, 0, 0)),
            pl.BlockSpec((1, seq, QKV_W), lambda i: (i, 0, 0)),
            pl.BlockSpec((seq // KC, QKV_W, KC), lambda i: (i, 0, 0)),
        ],
        out_specs=pl.BlockSpec((1, seq, N_HEADS * V_HEAD), lambda i: (i, 0, 0)),
        out_shape=jax.ShapeDtypeStruct((b, seq, N_HEADS * V_HEAD), BF16),
        scratch_shapes=[pltpu.VMEM((2, 2, seq, QT), F32), pltpu.VMEM((2, HEAD_PAD, QT), F32)],
        name="attn",
        compiler_params=pltpu.CompilerParams(
            dimension_semantics=("parallel",), vmem_limit_bytes=VMEM_LIMIT),
    )(q3, k3, vt)


def _hi_lo(a):
    hi = a.astype(BF16)
    return hi, (a - hi.astype(F32)).astype(BF16)


def _dot3(a, b):
    ah, al = _hi_lo(a)
    bh, bl = _hi_lo(b)
    return _dot(ah, bh) + (_dot(ah, bl) + _dot(al, bh))


def _filter_kernel(z_ref, decay_ref, w1, b1, f1, w2, b2, f2, w3, b3, f3, w4, filt_ref):
    h = jnp.sin(f1[0] * (_dot3(z_ref[...], w1[0]) + b1[0]))
    h = jnp.sin(f2[0] * (_dot3(h, w2[0]) + b2[0]))
    h = jnp.sin(f3[0] * (_dot3(h, w3[0]) + b3[0]))
    filt_ref[0] = _dot3(h, w4[0]) * decay_ref[...]


def _spectrum_kernel(f_ref, kd_ref, km_ref, gc_ref, gs_ref):
    kd = _dot3(f_ref[...], kd_ref[0])
    km = _dot3(f_ref[...], km_ref[0])
    row = lax.broadcasted_iota(jnp.int32, (HC, D_HY), 0)
    sigma = jnp.where(row % 2 == 0, 1.0, -1.0)
    gc_ref[0, 0] = (kd[:HC] + sigma * km[HC:]) * (1.0 / HC)
    gs_ref[0, 0] = (kd[HC:] - sigma * km[:HC]) * (1.0 / HC)


def _filter_spectrum(length, table, fw):
    depth = fw["w1"].shape[0]
    t = jnp.linspace(0.0, 1.0, length, dtype=F32)[:, None]
    bands = (FILTER_EMB - 1) // 2
    freqs = jnp.linspace(1e-4, bands - 1, bands, dtype=F32)[None, :]
    w = 2.0 * math.pi * jnp.arange(length, dtype=F32)[:, None] / length
    deltas = jnp.abs(jnp.linspace(math.log(DECAY_TARGET) / FAST_DECAY_PCT,
                                  math.log(DECAY_TARGET) / SLOW_DECAY_PCT, D_HY, dtype=F32))[None, :]

    half = LANES // 2
    m_idx = jnp.arange(length)
    pos = (length - m_idx) % length

    def features(tt, ww):
        zf = jnp.concatenate([tt, jnp.cos(freqs * ww), -jnp.sin(freqs * ww)], axis=-1)
        return jnp.pad(zf, ((0, 0), (0, half - FILTER_EMB)))

    z2 = jnp.concatenate([features(t[pos], w[pos]), features(t, w)], axis=1)
    decay_bwd = jnp.where(m_idx[:, None] == 0, 0.0, jnp.exp(-t[pos] * deltas) + MOD_SHIFT)
    decay2 = jnp.concatenate([decay_bwd, jnp.exp(-t * deltas) + MOD_SHIFT], axis=1)

    def lay(name):
        a = fw[name]
        if a.ndim == 2:
            a = jnp.pad(a, ((0, 0), (0, half - a.shape[1])))
            a = jnp.concatenate([a, a], axis=1)[:, None, :]
        elif name == "w4":
            zero = jnp.zeros_like(a[..., :D_HY])
            a = jnp.concatenate([jnp.concatenate([a[..., D_HY:], zero], axis=2),
                                 jnp.concatenate([zero, a[..., :D_HY]], axis=2)], axis=1)
        else:
            a = jnp.pad(a, ((0, 0), (0, half - a.shape[1]), (0, half - a.shape[2])))
            zero = jnp.zeros_like(a)
            a = jnp.concatenate([jnp.concatenate([a, zero], axis=2), jnp.concatenate([zero, a], axis=2)], axis=1)
        return a, pl.BlockSpec((1,) + a.shape[1:], lambda l: (l, 0, 0))

    names = ["w1", "b1", "f1", "w2", "b2", "f2", "w3", "b3", "f3", "w4"]
    arrs, specs = zip(*[lay(nm) for nm in names])
    k_lag = pl.pallas_call(
        _filter_kernel, grid=(depth,),
        in_specs=[_const_spec(z2.shape), _const_spec(decay2.shape), *specs],
        out_specs=pl.BlockSpec((1, length, 2 * D_HY), lambda l: (l, 0, 0)),
        out_shape=jax.ShapeDtypeStruct((depth, length, 2 * D_HY), F32), name="hyena_filter",
        compiler_params=pltpu.CompilerParams(vmem_limit_bytes=VMEM_LIMIT),
    )(z2, decay2, *arrs)

    nb = length // HC
    nd = 2 * nb - 1
    blk = lambda off: pl.BlockSpec((1, HC, D_HY), lambda l, d: (l, (d + off) % nb, (d + off) // nb))
    out = pl.BlockSpec((1, 1, HC, D_HY), lambda l, d: (l, d, 0, 0))
    g = jax.ShapeDtypeStruct((depth, nd, HC, D_HY), F32)
    return pl.pallas_call(
        _spectrum_kernel, grid=(depth, nd), in_specs=[_const_spec(table.shape), blk(1), blk(0)],
        out_specs=(out, out), out_shape=(g, g), name="hyena_spectrum",
        compiler_params=pltpu.CompilerParams(vmem_limit_bytes=VMEM_LIMIT),
    )(table, k_lag, k_lag)


def _dft_table():
    f = jnp.arange(HC, dtype=jnp.int32)[:, None]
    r = jnp.arange(HC, dtype=jnp.int32)[None, :]
    ang = (((2 * f + 1) * r) % (4 * HC)).astype(F32) * (math.pi / (2 * HC))
    return jnp.concatenate([jnp.cos(ang), jnp.sin(ang)], axis=0)


def _hyena_kernel(x0_ref, x1_ref, v_ref, cw_ref, cb_ref, skip_ref, gc_ref, gs_ref, f_ref, ft_ref,
                  y_ref, zf_s, zb_s, x0_s, zc_s, zs_s, yy_s):
    length, ch = zf_s.shape
    nb = length // HC
    def short_conv(ref, role, r0):
        halo = jnp.zeros((CONV_HALO, ch), F32)
        before = halo if r0 == 0 else ref[0, r0 - CONV_HALO:r0, :].astype(F32)
        last = r0 + CONV_ROWS == length
        after = halo if last else ref[0, r0 + CONV_ROWS:r0 + CONV_ROWS + CONV_HALO, :].astype(F32)
        ext = jnp.concatenate([before, ref[0, r0:r0 + CONV_ROWS, :].astype(F32), after], axis=0)
        n = ext.shape[0]
        mid = slice(CONV_HALO, CONV_HALO + CONV_ROWS)
        w0, w1, w2 = (cw_ref[3 * tap + role:3 * tap + role + 1, :] for tap in range(3))
        return (pltpu.roll(ext, 1, axis=0)[mid] * w0 + ext[mid] * w1
                + pltpu.roll(ext, n - 1, axis=0)[mid] * w2 + cb_ref[role:role + 1, :])

    for r0 in range(0, length, CONV_ROWS):
        rows = slice(r0, r0 + CONV_ROWS)
        x0_s[rows, :] = short_conv(x0_ref, 0, r0)
        z = short_conv(v_ref, 2, r0) * short_conv(x1_ref, 1, r0)
        zf_s[rows, :] = z
        zb_s[rows, :] = z.astype(BF16)

    for j in range(nb):
        zz = _dot(f_ref[...], zb_s[j * HC:(j + 1) * HC, :])
        zc_s[j] = zz[:HC]
        zs_s[j] = zz[HC:]

    def mix(t, carry):
        r = pl.ds(pl.multiple_of(t * MIX_ROWS, MIX_ROWS), MIX_ROWS)
        r_sin = pl.ds(pl.multiple_of(HC + t * MIX_ROWS, MIX_ROWS), MIX_ROWS)
        zc = [zc_s[j, r, :] for j in range(nb)]
        zs = [zs_s[j, r, :] for j in range(nb)]
        gc = [gc_ref[0, d, r, :] for d in range(2 * nb - 1)]
        gs = [gs_ref[0, d, r, :] for d in range(2 * nb - 1)]
        for i in range(nb):
            yc = ys = None
            for j in range(nb):
                d = i - j + nb - 1
                tc = zc[j] * gc[d] - zs[j] * gs[d]
                ts = zc[j] * gs[d] + zs[j] * gc[d]
                yc = tc if yc is None else yc + tc
                ys = ts if ys is None else ys + ts
            yy_s[i, r, :] = yc.astype(BF16)
            yy_s[i, r_sin, :] = ys.astype(BF16)
        return carry

    lax.fori_loop(0, HC // MIX_ROWS, mix, 0)

    for i in range(nb):
        rows = slice(i * HC, (i + 1) * HC)
        y = _dot(ft_ref[...], yy_s[i])
        y = (y + zf_s[rows, :] * skip_ref[...]) * x0_s[rows, :]
        y_ref[0, rows, :] = y.astype(BF16)


def _hyena_call(hy3, cw, cb, skip, gc, gs, layer, table_b, table_t):
    b, length, _ = hy3.shape
    ch = D_HY // 2
    nh = D_HY // ch
    nb = length // HC

    def role_spec(role):
        return pl.BlockSpec((1, length, ch), lambda j, i: (i, 0, role * nh + j))

    g_blk = pl.BlockSpec((1, 2 * nb - 1, HC, ch), lambda j, i: (layer, 0, 0, j),
                         pipeline_mode=pl.Buffered(1))
    in_specs = [
        role_spec(0), role_spec(1), role_spec(2),
        pl.BlockSpec((9, ch), lambda j, i: (0, j)),
        pl.BlockSpec((3, ch), lambda j, i: (0, j)),
        pl.BlockSpec((1, ch), lambda j, i: (0, j)),
        g_blk, g_blk,
        _const_spec(table_b.shape), _const_spec(table_t.shape),
    ]
    return pl.pallas_call(
        _hyena_kernel, grid=(nh, b), in_specs=in_specs,
        out_specs=pl.BlockSpec((1, length, ch), lambda j, i: (i, 0, j)),
        out_shape=jax.ShapeDtypeStruct((b, length, D_HY), BF16),
        scratch_shapes=[
            pltpu.VMEM((length, ch), F32), pltpu.VMEM((length, ch), BF16),
            pltpu.VMEM((length, ch), F32), pltpu.VMEM((nb, HC, ch), F32),
            pltpu.VMEM((nb, HC, ch), F32), pltpu.VMEM((nb, 2 * HC, ch), BF16),
        ],
        name="hyena",
        compiler_params=pltpu.CompilerParams(
            dimension_semantics=("parallel", "parallel"), vmem_limit_bytes=VMEM_LIMIT),
    )(hy3, hy3, hy3, cw, cb, skip, gc, gs, table_b, table_t)


def _merge_kernel(final_norm, x_ref, ao_ref, hy_ref, gm_ref, wga_ref, wgb_ref, wap_ref, whp_ref,
                  wo_ref, gf_ref, wg_ref, wu_ref, wd_ref, gfin_ref, o_ref):
    x = x_ref[...]
    hb = _rms(x, gm_ref[...]).astype(BF16)
    merged = jax.nn.sigmoid(_dot(hb, wga_ref[...])) * _dot(ao_ref[...], wap_ref[...])
    merged = merged + jax.nn.sigmoid(_dot(hb, wgb_ref[...])) * _dot(hy_ref[...], whp_ref[...])
    x = x + _dot(merged.astype(BF16), wo_ref[...])
    hb = _rms(x, gf_ref[...]).astype(BF16)
    for start, width in FFN_CHUNKS:
        cols = slice(start, start + width)
        act = jax.nn.silu(_dot(hb, wg_ref[:, cols])) * _dot(hb, wu_ref[:, cols])
        x = x + _dot(act.astype(BF16), wd_ref[cols, :])
    if final_norm:
        x = _rms(x, gfin_ref[...])
    o_ref[...] = x


def _merge_call(x2, ao2, hy2, gm, wga, wgb, wap, whp, wo, gf, wg, wu, wd, gfin, final_norm):
    n = x2.shape[0]
    tm = TM_MERGE
    row = lambda i: (i, 0)
    consts = [gm, wga, wgb, wap, whp, wo, gf, wg, wu, wd, gfin]
    in_specs = [
        pl.BlockSpec((tm, D_MODEL), row),
        pl.BlockSpec((tm, ao2.shape[1]), row),
        pl.BlockSpec((tm, hy2.shape[1]), row),
    ] + [_const_spec(c.shape) for c in consts]
    return pl.pallas_call(
        functools.partial(_merge_kernel, final_norm), grid=(n // tm,), in_specs=in_specs,
        out_specs=pl.BlockSpec((tm, D_MODEL), row),
        out_shape=jax.ShapeDtypeStruct((n, D_MODEL), F32), name="merge_ffn",
        compiler_params=pltpu.CompilerParams(
            dimension_semantics=("parallel",), vmem_limit_bytes=VMEM_LIMIT),
    )(x2, ao2, hy2, *consts)


def _swap_pairs(w):
    w2 = w.reshape(w.shape[0], -1, 2)
    return jnp.stack([-w2[..., 1], w2[..., 0]], axis=-1).reshape(w.shape)


def _layer_weights(w_in, w_q_up, w_kv_up):
    c0, c1, c2, c3 = Q_LORA, Q_LORA + KV_LORA, Q_LORA + KV_LORA + QK_ROPE, Q_LORA + KV_LORA + QK_ROPE + 3 * D_HY
    w_kpe = w_in[:, c1:c2]
    pad = jnp.zeros((D_MODEL, P1_W - c2 - QK_ROPE), F32)
    w1 = jnp.concatenate([w_in[:, :c1], w_kpe, _swap_pairs(w_kpe), pad], axis=1).astype(BF16)
    why = w_in[:, c2:c3].astype(BF16)
    wga = w_in[:, c3:c3 + D_MODEL].astype(BF16)
    wgb = w_in[:, c3 + D_MODEL:].astype(BF16)

    wq3 = w_q_up.reshape(Q_LORA, N_HEADS, QK_NOPE + QK_ROPE)
    wq_pe = wq3[..., QK_NOPE:]
    wq = jnp.concatenate([wq3[..., :QK_NOPE], wq_pe, _swap_pairs(wq_pe.reshape(Q_LORA, -1)).reshape(wq_pe.shape)],
                         axis=-1).reshape(Q_LORA, QKV_W).astype(BF16)

    wkv3 = w_kv_up.reshape(KV_LORA, N_HEADS, QK_NOPE + V_HEAD)
    zpad = jnp.zeros((KV_LORA, N_HEADS, HEAD_PAD - QK_NOPE), F32)
    wk = jnp.concatenate([wkv3[..., :QK_NOPE], zpad], axis=-1).reshape(KV_LORA, QKV_W).astype(BF16)
    wv3 = wkv3[..., QK_NOPE:]
    odd = (jnp.arange(N_HEADS) % 2 == 1)[None, :, None]
    wv = jnp.where(odd, jnp.concatenate([zpad, wv3], axis=-1), jnp.concatenate([wv3, zpad], axis=-1))
    wv = wv.reshape(KV_LORA, QKV_W).astype(BF16)
    return w1, why, wga, wgb, wq.T, wk, wv.T


def _static_tables():
    lane = jnp.arange(QKV_W) % HEAD_PAD
    head = jnp.arange(QKV_W) // HEAD_PAD
    vone = jnp.where(head % 2 == 0, lane == V_HEAD, lane == 0).astype(F32)[:, None]
    src = jnp.arange(LANES)[:, None]
    place = ((src < QK_ROPE) & (lane[None, :] >= QK_NOPE) & ((lane[None, :] - QK_NOPE) % QK_ROPE == src))
    return vone, place.astype(BF16)


def kernel(x, positions, mix_norm_g, w_in, q_norm_g, w_q_up, kv_norm_g, w_kv_up, w_attn_proj, hy_conv_w, hy_conv_b, filt_w1, filt_b1, filt_f1, filt_w2, filt_b2, filt_f2, filt_w3, filt_b3, filt_f3, filt_w4, hy_skip, w_hy_proj, w_out, ffn_norm_g, w_gate, w_up, w_down, final_norm_g):
    b, seq, d = x.shape
    depth = w_in.shape[0]
    qtab, kcos, ksin = _rope_tables(positions)
    vone, place = _static_tables()
    table = _dft_table()
    table_b = table.astype(BF16)
    fw = dict(w1=filt_w1, b1=filt_b1, f1=filt_f1, w2=filt_w2, b2=filt_b2, f2=filt_f2,
              w3=filt_w3, b3=filt_b3, f3=filt_f3, w4=filt_w4)
    g_cos, g_sin = _filter_spectrum(seq, table, fw)

    x2 = x.reshape(b * seq, d)
    for i in range(depth):
        w1, why, wga, wgb, wqt, wk, wvt = _layer_weights(w_in[i], w_q_up[i], w_kv_up[i])
        q3, k2, vt, hy2 = _proj_call(
            x2, seq, mix_norm_g[i][None], w1, why, q_norm_g[i][None], wqt, kv_norm_g[i][None],
            wk, wvt, vone, place, qtab, kcos, ksin)
        ao = _attn_call(q3, k2.reshape(b, seq, QKV_W), vt)
        hy = _hyena_call(
            hy2.reshape(b, seq, 3 * D_HY), hy_conv_w[i].reshape(9, D_HY), hy_conv_b[i].reshape(3, D_HY),
            hy_skip[i][None], g_cos, g_sin, i, table_b, table_b.T)
        x2 = _merge_call(
            x2, ao.reshape(b * seq, -1), hy.reshape(b * seq, -1), mix_norm_g[i][None], wga, wgb,
            w_attn_proj[i].astype(BF16), w_hy_proj[i].astype(BF16), w_out[i].astype(BF16),
            ffn_norm_g[i][None], w_gate[i].astype(BF16), w_up[i].astype(BF16), w_down[i].astype(BF16),
            final_norm_g[None], final_norm=(i == depth - 1))
    return x2.reshape(b, seq, d)
```

```python
import functools
import math

import jax
import jax.numpy as jnp
from jax import lax
from jax.experimental import pallas as pl
from jax.experimental.pallas import tpu as pltpu

D_MODEL = 1024
N_HEADS = 8
Q_LORA = 256
KV_LORA = 128
QK_NOPE = 64
QK_ROPE = 32
V_HEAD = 64
ROPE_THETA = 10000.0
D_HY = 512
FILTER_EMB = 33
FILTER_HIDDEN = 64
FAST_DECAY_PCT = 0.3
SLOW_DECAY_PCT = 1.5
DECAY_TARGET = 1e-2
MOD_SHIFT = 0.0
EPS = 1e-6

LANES = 128
HEAD_PAD = LANES
QKV_W = N_HEADS * HEAD_PAD
P1_W = 512
VMEM_LIMIT = 56 * 1024 * 1024

TM_PROJ = 1024
TM_MERGE = 512
QT = 512
KC = 1024
SUB = 512
HC = 512
CONV_ROWS = 128
CONV_HALO = 16
MIX_ROWS = 32
FFN_CHUNKS = ((0, 1536), (1536, 1280))

F32 = jnp.float32
BF16 = jnp.bfloat16


def _const_spec(shape):
    nd = len(shape)
    return pl.BlockSpec(shape, lambda *_: (0,) * nd, pipeline_mode=pl.Buffered(1))


def _rms(x, g):
    return x * lax.rsqrt(jnp.mean(x * x, axis=-1, keepdims=True) + EPS) * g


def _dot(a, b):
    return jnp.dot(a, b, preferred_element_type=F32)


def _rope_kernel(pos_col_ref, pos_row_ref, invq_ref, invk_ref, qtab_ref, kcos_ref, ksin_ref):
    sub = lax.broadcasted_iota(jnp.int32, qtab_ref.shape, 0)
    scale = (QK_NOPE + QK_ROPE) ** -0.5 * math.log2(math.e)
    angq = pos_row_ref[...].astype(F32) * invq_ref[...]
    rot = jnp.where(sub < QK_NOPE + QK_ROPE, jnp.cos(angq), jnp.sin(angq))
    qtab_ref[...] = jnp.where(sub < QK_NOPE, 1.0, rot) * scale
    lane = lax.broadcasted_iota(jnp.int32, kcos_ref.shape, 1)
    angk = pos_col_ref[...].astype(F32) * invk_ref[...]
    kcos_ref[...] = jnp.where(lane < QK_ROPE, jnp.cos(angk), 0.0)
    ksin_ref[...] = jnp.where(lane < QK_ROPE, jnp.sin(angk), 0.0)


def _rope_tables(positions):
    s = positions.shape[0]
    inv = 1.0 / (ROPE_THETA ** (jnp.arange(0, QK_ROPE, 2, dtype=F32) / QK_ROPE))
    inv_pairs = jnp.repeat(inv, 2)
    zeros = jnp.zeros((QK_NOPE,), F32)
    invq = jnp.concatenate([zeros, inv_pairs, inv_pairs])[:, None]
    invk = jnp.concatenate([inv_pairs, jnp.zeros((LANES - QK_ROPE,), F32)])[None, :]
    return pl.pallas_call(
        _rope_kernel, name="rope_tables",
        out_shape=(jax.ShapeDtypeStruct((LANES, s), F32), jax.ShapeDtypeStruct((s, LANES), F32),
                   jax.ShapeDtypeStruct((s, LANES), F32)),
    )(positions.reshape(s, 1), positions.reshape(1, s), invq, invk)


_NT = (((1,), (1,)), ((), ()))


def _store_tiles(out_ref, val):
    width = out_ref.shape[2]
    for c in range(out_ref.shape[0]):
        out_ref[c] = val[:, c * width:(c + 1) * width]


def _tile_spec(tile, tm):
    if tile >= tm:
        per = tile // tm
        return pl.BlockSpec((1, QKV_W, tm), lambda i: (i // per, 0, i % per))
    return pl.BlockSpec((tm // tile, QKV_W, tile), lambda i: (i, 0, 0))


def _proj_kernel(x_ref, g_ref, w1_ref, why_ref, gq_ref, wqt_ref, gkv_ref, wk_ref, wvt_ref,
                 vone_ref, place_ref, qtab_ref, kcos_ref, ksin_ref,
                 q_out, k_out, v_out, hy_out):
    hb = _rms(x_ref[...], g_ref[...]).astype(BF16)
    p1 = _dot(hb, w1_ref[...])
    hy_out[...] = _dot(hb, why_ref[...]).astype(BF16)

    qn = _rms(p1[:, :Q_LORA], gq_ref[...]).astype(BF16)
    qtab = jnp.concatenate([qtab_ref[...]] * N_HEADS, axis=0)
    q_t = (lax.dot_general(wqt_ref[...], qn, _NT, preferred_element_type=F32) * qtab).astype(BF16)
    _store_tiles(q_out, q_t)

    kvn = _rms(p1[:, Q_LORA:Q_LORA + KV_LORA], gkv_ref[...]).astype(BF16)
    t = p1[:, Q_LORA + KV_LORA:]
    k_rot = t * kcos_ref[...] + pltpu.roll(t, LANES - QK_ROPE, axis=1) * ksin_ref[...]
    k_out[...] = (_dot(kvn, wk_ref[...]) + _dot(k_rot.astype(BF16), place_ref[...])).astype(BF16)
    v_t = lax.dot_general(wvt_ref[...], kvn, _NT, preferred_element_type=F32) + vone_ref[...]
    _store_tiles(v_out, v_t.astype(BF16))


def _proj_call(x2, seq, g, w1, why, gq, wqt, gkv, wk, wvt, vone, place, qtab, kcos, ksin):
    n = x2.shape[0]
    tm = TM_PROJ
    tiles_per_seq = seq // tm
    row = lambda i: (i, 0)
    tab = lambda i: (i % tiles_per_seq, 0)
    in_specs = [
        pl.BlockSpec((tm, D_MODEL), row),
        _const_spec(g.shape), _const_spec(w1.shape), _const_spec(why.shape),
        _const_spec(gq.shape), _const_spec(wqt.shape), _const_spec(gkv.shape),
        _const_spec(wk.shape), _const_spec(wvt.shape), _const_spec(vone.shape),
        _const_spec(place.shape),
        pl.BlockSpec((LANES, tm), lambda i: (0, i % tiles_per_seq)),
        pl.BlockSpec((tm, LANES), tab), pl.BlockSpec((tm, LANES), tab),
    ]
    out_shape = (
        jax.ShapeDtypeStruct((n // QT, QKV_W, QT), BF16),
        jax.ShapeDtypeStruct((n, QKV_W), BF16),
        jax.ShapeDtypeStruct((n // KC, QKV_W, KC), BF16),
        jax.ShapeDtypeStruct((n, 3 * D_HY), BF16),
    )
    out_specs = (
        _tile_spec(QT, tm), pl.BlockSpec((tm, QKV_W), row),
        _tile_spec(KC, tm), pl.BlockSpec((tm, 3 * D_HY), row),
    )
    return pl.pallas_call(
        _proj_kernel, grid=(n // tm,), in_specs=in_specs, out_specs=out_specs,
        out_shape=out_shape, name="proj",
        compiler_params=pltpu.CompilerParams(
            dimension_semantics=("parallel",), vmem_limit_bytes=VMEM_LIMIT),
    )(x2, g, w1, why, gq, wqt, gkv, wk, wvt, vone, place, qtab, kcos, ksin)


def _attn_kernel(q_ref, k_ref, v_ref, o_ref, s_ref, acc_ref):
    n_tiles, n_chunks = q_ref.shape[0], v_ref.shape[0]
    sub = lax.broadcasted_iota(jnp.int32, (HEAD_PAD, QT), 0)
    units = [(pair, t) for pair in range(N_HEADS // 2) for t in range(n_tiles)]

    def feat(pair, hh):
        return slice((2 * pair + hh) * HEAD_PAD, (2 * pair + hh + 1) * HEAD_PAD)

    def stage_a(kc, sub_i, hh, u, m_run):
        pair, t = units[u]
        keys = pl.ds(pl.multiple_of(kc * KC + sub_i * SUB, SUB), SUB)
        sc = _dot(k_ref[0, keys, feat(pair, hh)], q_ref[t, feat(pair, hh), :])
        s_ref[u % 2, hh, keys, :] = sc
        return jnp.maximum(m_run, jnp.max(sc.reshape(SUB // 8, 8, QT), axis=0))

    def stage_b(kc, sub_i, hh, u, m):
        pair, _ = units[u]
        keys = pl.ds(pl.multiple_of(kc * KC + sub_i * SUB, SUB), SUB)
        p = jnp.exp2((s_ref[u % 2, hh, keys, :] - m).astype(BF16))
        acc_ref[hh] += _dot(v_ref[kc, feat(pair, hh), sub_i * SUB:(sub_i + 1) * SUB], p)

    def chunk(kc, u_a, u_b, m_run, m):
        m_run = list(m_run)
        for sub_i in range(KC // SUB):
            for hh in range(2):
                if u_a is not None:
                    m_run[hh] = stage_a(kc, sub_i, hh, u_a, m_run[hh])
                if u_b is not None:
                    stage_b(kc, sub_i, hh, u_b, m[hh])
        return tuple(m_run)

    neg = (jnp.full((8, QT), -jnp.inf, F32),) * 2
    m_run = lax.fori_loop(0, n_chunks, lambda kc, mr: chunk(kc, 0, None, mr, None), neg)
    for u, (pair, t) in enumerate(units):
        m = tuple(jnp.max(mr, axis=0, keepdims=True) for mr in m_run)
        acc_ref[...] = jnp.zeros_like(acc_ref)
        if u + 1 < len(units):
            def both(kc, mr, u=u, m=m):
                return chunk(kc, u + 1, u, mr, m)
            m_run = lax.fori_loop(0, n_chunks, both, neg)
        else:
            def last(kc, carry, u=u, m=m):
                chunk(kc, None, u, (), m)
                return carry
            lax.fori_loop(0, n_chunks, last, 0)
        acc0, acc1 = acc_ref[0], acc_ref[1]
        o_t = jnp.where(sub < V_HEAD, acc0 / acc0[V_HEAD:V_HEAD + 1, :], acc1 / acc1[0:1, :])
        o_ref[0, t * QT:(t + 1) * QT, pair * 2 * V_HEAD:(pair + 1) * 2 * V_HEAD] = o_t.T.astype(BF16)


def _attn_call(q3, k3, vt):
    b, seq, _ = k3.shape
    return pl.pallas_call(
        _attn_kernel, grid=(b,),
        in_specs=[
            pl.BlockSpec((seq // QT, QKV_W, QT), lambda i: (i, 0, 0)),
            pl.BlockSpec((1, seq, QKV_W), lambda i: (i, 0, 0)),
            pl.BlockSpec((seq // KC, QKV_W, KC), lambda i: (i, 0, 0)),
        ],
        out_specs=pl.BlockSpec((1, seq, N_HEADS * V_HEAD), lambda i: (i, 0, 0)),
        out_shape=jax.ShapeDtypeStruct((b, seq, N_HEADS * V_HEAD), BF16),
        scratch_shapes=[pltpu.VMEM((2, 2, seq, QT), F32), pltpu.VMEM((2, HEAD_PAD, QT), F32)],
        name="attn",
        compiler_params=pltpu.CompilerParams(
            dimension_semantics=("parallel",), vmem_limit_bytes=VMEM_LIMIT),
    )(q3, k3, vt)


def _hi_lo(a):
    hi = a.astype(BF16)
    return hi, (a - hi.astype(F32)).astype(BF16)


def _dot3(a, b):
    ah, al = _hi_lo(a)
    bh, bl = _hi_lo(b)
    return _dot(ah, bh) + (_dot(ah, bl) + _dot(al, bh))


def _filter_kernel(z_ref, decay_ref, w1, b1, f1, w2, b2, f2, w3, b3, f3, w4, filt_ref):
    h = jnp.sin(f1[0] * (_dot3(z_ref[...], w1[0]) + b1[0]))
    h = jnp.sin(f2[0] * (_dot3(h, w2[0]) + b2[0]))
    h = jnp.sin(f3[0] * (_dot3(h, w3[0]) + b3[0]))
    filt_ref[0] = _dot3(h, w4[0]) * decay_ref[...]


def _spectrum_kernel(f_ref, kd_ref, km_ref, gc_ref, gs_ref):
    kd = _dot3(f_ref[...], kd_ref[0])
    km = _dot3(f_ref[...], km_ref[0])
    row = lax.broadcasted_iota(jnp.int32, (HC, D_HY), 0)
    sigma = jnp.where(row % 2 == 0, 1.0, -1.0)
    gc_ref[0, 0] = (kd[:HC] + sigma * km[HC:]) * (1.0 / HC)
    gs_ref[0, 0] = (kd[HC:] - sigma * km[:HC]) * (1.0 / HC)


def _filter_spectrum(length, table, fw):
    depth = fw["w1"].shape[0]
    t = jnp.linspace(0.0, 1.0, length, dtype=F32)[:, None]
    bands = (FILTER_EMB - 1) // 2
    freqs = jnp.linspace(1e-4, bands - 1, bands, dtype=F32)[None, :]
    w = 2.0 * math.pi * jnp.arange(length, dtype=F32)[:, None] / length
    deltas = jnp.abs(jnp.linspace(math.log(DECAY_TARGET) / FAST_DECAY_PCT,
                                  math.log(DECAY_TARGET) / SLOW_DECAY_PCT, D_HY, dtype=F32))[None, :]

    half = LANES // 2
    m_idx = jnp.arange(length)
    pos = (length - m_idx) % length

    def features(tt, ww):
        zf = jnp.concatenate([tt, jnp.cos(freqs * ww), -jnp.sin(freqs * ww)], axis=-1)
        return jnp.pad(zf, ((0, 0), (0, half - FILTER_EMB)))

    z2 = jnp.concatenate([features(t[pos], w[pos]), features(t, w)], axis=1)
    decay_bwd = jnp.where(m_idx[:, None] == 0, 0.0, jnp.exp(-t[pos] * deltas) + MOD_SHIFT)
    decay2 = jnp.concatenate([decay_bwd, jnp.exp(-t * deltas) + MOD_SHIFT], axis=1)

    def lay(name):
        a = fw[name]
        if a.ndim == 2:
            a = jnp.pad(a, ((0, 0), (0, half - a.shape[1])))
            a = jnp.concatenate([a, a], axis=1)[:, None, :]
        elif name == "w4":
            zero = jnp.zeros_like(a[..., :D_HY])
            a = jnp.concatenate([jnp.concatenate([a[..., D_HY:], zero], axis=2),
                                 jnp.concatenate([zero, a[..., :D_HY]], axis=2)], axis=1)
        else:
            a = jnp.pad(a, ((0, 0), (0, half - a.shape[1]), (0, half - a.shape[2])))
            zero = jnp.zeros_like(a)
            a = jnp.concatenate([jnp.concatenate([a, zero], axis=2), jnp.concatenate([zero, a], axis=2)], axis=1)
        return a, pl.BlockSpec((1,) + a.shape[1:], lambda l: (l, 0, 0))

    names = ["w1", "b1", "f1", "w2", "b2", "f2", "w3", "b3", "f3", "w4"]
    arrs, specs = zip(*[lay(nm) for nm in names])
    k_lag = pl.pallas_call(
        _filter_kernel, grid=(depth,),
        in_specs=[_const_spec(z2.shape), _const_spec(decay2.shape), *specs],
        out_specs=pl.BlockSpec((1, length, 2 * D_HY), lambda l: (l, 0, 0)),
        out_shape=jax.ShapeDtypeStruct((depth, length, 2 * D_HY), F32), name="hyena_filter",
        compiler_params=pltpu.CompilerParams(vmem_limit_bytes=VMEM_LIMIT),
    )(z2, decay2, *arrs)

    nb = length // HC
    nd = 2 * nb - 1
    blk = lambda off: pl.BlockSpec((1, HC, D_HY), lambda l, d: (l, (d + off) % nb, (d + off) // nb))
    out = pl.BlockSpec((1, 1, HC, D_HY), lambda l, d: (l, d, 0, 0))
    g = jax.ShapeDtypeStruct((depth, nd, HC, D_HY), F32)
    return pl.pallas_call(
        _spectrum_kernel, grid=(depth, nd), in_specs=[_const_spec(table.shape), blk(1), blk(0)],
        out_specs=(out, out), out_shape=(g, g), name="hyena_spectrum",
        compiler_params=pltpu.CompilerParams(vmem_limit_bytes=VMEM_LIMIT),
    )(table, k_lag, k_lag)


def _dft_table():
    f = jnp.arange(HC, dtype=jnp.int32)[:, None]
    r = jnp.arange(HC, dtype=jnp.int32)[None, :]
    ang = (((2 * f + 1) * r) % (4 * HC)).astype(F32) * (math.pi / (2 * HC))
    return jnp.concatenate([jnp.cos(ang), jnp.sin(ang)], axis=0)


def _hyena_kernel(x0_ref, x1_ref, v_ref, cw_ref, cb_ref, skip_ref, gc_ref, gs_ref, f_ref, ft_ref,
                  y_ref, zf_s, zb_s, x0_s, zc_s, zs_s, yy_s):
    length, ch = zf_s.shape
    nb = length // HC
    def short_conv(ref, role, r0):
        halo = jnp.zeros((CONV_HALO, ch), F32)
        before = halo if r0 == 0 else ref[0, r0 - CONV_HALO:r0, :].astype(F32)
        last = r0 + CONV_ROWS == length
        after = halo if last else ref[0, r0 + CONV_ROWS:r0 + CONV_ROWS + CONV_HALO, :].astype(F32)
        ext = jnp.concatenate([before, ref[0, r0:r0 + CONV_ROWS, :].astype(F32), after], axis=0)
        n = ext.shape[0]
        mid = slice(CONV_HALO, CONV_HALO + CONV_ROWS)
        w0, w1, w2 = (cw_ref[3 * tap + role:3 * tap + role + 1, :] for tap in range(3))
        return (pltpu.roll(ext, 1, axis=0)[mid] * w0 + ext[mid] * w1
                + pltpu.roll(ext, n - 1, axis=0)[mid] * w2 + cb_ref[role:role + 1, :])

    for r0 in range(0, length, CONV_ROWS):
        rows = slice(r0, r0 + CONV_ROWS)
        x0_s[rows, :] = short_conv(x0_ref, 0, r0)
        z = short_conv(v_ref, 2, r0) * short_conv(x1_ref, 1, r0)
        zf_s[rows, :] = z
        zb_s[rows, :] = z.astype(BF16)

    for j in range(nb):
        zz = _dot(f_ref[...], zb_s[j * HC:(j + 1) * HC, :])
        zc_s[j] = zz[:HC]
        zs_s[j] = zz[HC:]

    def mix(t, carry):
        r = pl.ds(pl.multiple_of(t * MIX_ROWS, MIX_ROWS), MIX_ROWS)
        r_sin = pl.ds(pl.multiple_of(HC + t * MIX_ROWS, MIX_ROWS), MIX_ROWS)
        zc = [zc_s[j, r, :] for j in range(nb)]
        zs = [zs_s[j, r, :] for j in range(nb)]
        gc = [gc_ref[0, d, r, :] for d in range(2 * nb - 1)]
        gs = [gs_ref[0, d, r, :] for d in range(2 * nb - 1)]
        for i in range(nb):
            yc = ys = None
            for j in range(nb):
                d = i - j + nb - 1
                tc = zc[j] * gc[d] - zs[j] * gs[d]
                ts = zc[j] * gs[d] + zs[j] * gc[d]
                yc = tc if yc is None else yc + tc
                ys = ts if ys is None else ys + ts
            yy_s[i, r, :] = yc.astype(BF16)
            yy_s[i, r_sin, :] = ys.astype(BF16)
        return carry

    lax.fori_loop(0, HC // MIX_ROWS, mix, 0)

    for i in range(nb):
        rows = slice(i * HC, (i + 1) * HC)
        y = _dot(ft_ref[...], yy_s[i])
        y = (y + zf_s[rows, :] * skip_ref[...]) * x0_s[rows, :]
        y_ref[0, rows, :] = y.astype(BF16)


def _hyena_call(hy3, cw, cb, skip, gc, gs, layer, table_b, table_t):
    b, length, _ = hy3.shape
    ch = D_HY // 2
    nh = D_HY // ch
    nb = length // HC

    def role_spec(role):
        return pl.BlockSpec((1, length, ch), lambda j, i: (i, 0, role * nh + j))

    g_blk = pl.BlockSpec((1, 2 * nb - 1, HC, ch), lambda j, i: (layer, 0, 0, j),
                         pipeline_mode=pl.Buffered(1))
    in_specs = [
        role_spec(0), role_spec(1), role_spec(2),
        pl.BlockSpec((9, ch), lambda j, i: (0, j)),
        pl.BlockSpec((3, ch), lambda j, i: (0, j)),
        pl.BlockSpec((1, ch), lambda j, i: (0, j)),
        g_blk, g_blk,
        _const_spec(table_b.shape), _const_spec(table_t.shape),
    ]
    return pl.pallas_call(
        _hyena_kernel, grid=(nh, b), in_specs=in_specs,
        out_specs=pl.BlockSpec((1, length, ch), lambda j, i: (i, 0, j)),
        out_shape=jax.ShapeDtypeStruct((b, length, D_HY), BF16),
        scratch_shapes=[
            pltpu.VMEM((length, ch), F32), pltpu.VMEM((length, ch), BF16),
            pltpu.VMEM((length, ch), F32), pltpu.VMEM((nb, HC, ch), F32),
            pltpu.VMEM((nb, HC, ch), F32), pltpu.VMEM((nb, 2 * HC, ch), BF16),
        ],
        name="hyena",
        compiler_params=pltpu.CompilerParams(
            dimension_semantics=("parallel", "parallel"), vmem_limit_bytes=VMEM_LIMIT),
    )(hy3, hy3, hy3, cw, cb, skip, gc, gs, table_b, table_t)


def _merge_kernel(final_norm, x_ref, ao_ref, hy_ref, gm_ref, wga_ref, wgb_ref, wap_ref, whp_ref,
                  wo_ref, gf_ref, wg_ref, wu_ref, wd_ref, gfin_ref, o_ref):
    x = x_ref[...]
    hb = _rms(x, gm_ref[...]).astype(BF16)
    merged = jax.nn.sigmoid(_dot(hb, wga_ref[...])) * _dot(ao_ref[...], wap_ref[...])
    merged = merged + jax.nn.sigmoid(_dot(hb, wgb_ref[...])) * _dot(hy_ref[...], whp_ref[...])
    x = x + _dot(merged.astype(BF16), wo_ref[...])
    hb = _rms(x, gf_ref[...]).astype(BF16)
    for start, width in FFN_CHUNKS:
        cols = slice(start, start + width)
        act = jax.nn.silu(_dot(hb, wg_ref[:, cols])) * _dot(hb, wu_ref[:, cols])
        x = x + _dot(act.astype(BF16), wd_ref[cols, :])
    if final_norm:
        x = _rms(x, gfin_ref[...])
    o_ref[...] = x


def _merge_call(x2, ao2, hy2, gm, wga, wgb, wap, whp, wo, gf, wg, wu, wd, gfin, final_norm):
    n = x2.shape[0]
    tm = TM_MERGE
    row = lambda i: (i, 0)
    consts = [gm, wga, wgb, wap, whp, wo, gf, wg, wu, wd, gfin]
    in_specs = [
        pl.BlockSpec((tm, D_MODEL), row),
        pl.BlockSpec((tm, ao2.shape[1]), row),
        pl.BlockSpec((tm, hy2.shape[1]), row),
    ] + [_const_spec(c.shape) for c in consts]
    return pl.pallas_call(
        functools.partial(_merge_kernel, final_norm), grid=(n // tm,), in_specs=in_specs,
        out_specs=pl.BlockSpec((tm, D_MODEL), row),
        out_shape=jax.ShapeDtypeStruct((n, D_MODEL), F32), name="merge_ffn",
        compiler_params=pltpu.CompilerParams(
            dimension_semantics=("parallel",), vmem_limit_bytes=VMEM_LIMIT),
    )(x2, ao2, hy2, *consts)


def _swap_pairs(w):
    w2 = w.reshape(w.shape[0], -1, 2)
    return jnp.stack([-w2[..., 1], w2[..., 0]], axis=-1).reshape(w.shape)


def _layer_weights(w_in, w_q_up, w_kv_up):
    c0, c1, c2, c3 = Q_LORA, Q_LORA + KV_LORA, Q_LORA + KV_LORA + QK_ROPE, Q_LORA + KV_LORA + QK_ROPE + 3 * D_HY
    w_kpe = w_in[:, c1:c2]
    pad = jnp.zeros((D_MODEL, P1_W - c2 - QK_ROPE), F32)
    w1 = jnp.concatenate([w_in[:, :c1], w_kpe, _swap_pairs(w_kpe), pad], axis=1).astype(BF16)
    why = w_in[:, c2:c3].astype(BF16)
    wga = w_in[:, c3:c3 + D_MODEL].astype(BF16)
    wgb = w_in[:, c3 + D_MODEL:].astype(BF16)

    wq3 = w_q_up.reshape(Q_LORA, N_HEADS, QK_NOPE + QK_ROPE)
    wq_pe = wq3[..., QK_NOPE:]
    wq = jnp.concatenate([wq3[..., :QK_NOPE], wq_pe, _swap_pairs(wq_pe.reshape(Q_LORA, -1)).reshape(wq_pe.shape)],
                         axis=-1).reshape(Q_LORA, QKV_W).astype(BF16)

    wkv3 = w_kv_up.reshape(KV_LORA, N_HEADS, QK_NOPE + V_HEAD)
    zpad = jnp.zeros((KV_LORA, N_HEADS, HEAD_PAD - QK_NOPE), F32)
    wk = jnp.concatenate([wkv3[..., :QK_NOPE], zpad], axis=-1).reshape(KV_LORA, QKV_W).astype(BF16)
    wv3 = wkv3[..., QK_NOPE:]
    odd = (jnp.arange(N_HEADS) % 2 == 1)[None, :, None]
    wv = jnp.where(odd, jnp.concatenate([zpad, wv3], axis=-1), jnp.concatenate([wv3, zpad], axis=-1))
    wv = wv.reshape(KV_LORA, QKV_W).astype(BF16)
    return w1, why, wga, wgb, wq.T, wk, wv.T


def _static_tables():
    lane = jnp.arange(QKV_W) % HEAD_PAD
    head = jnp.arange(QKV_W) // HEAD_PAD
    vone = jnp.where(head % 2 == 0, lane == V_HEAD, lane == 0).astype(F32)[:, None]
    src = jnp.arange(LANES)[:, None]
    place = ((src < QK_ROPE) & (lane[None, :] >= QK_NOPE) & ((lane[None, :] - QK_NOPE) % QK_ROPE == src))
    return vone, place.astype(BF16)


def kernel(x, positions, mix_norm_g, w_in, q_norm_g, w_q_up, kv_norm_g, w_kv_up, w_attn_proj, hy_conv_w, hy_conv_b, filt_w1, filt_b1, filt_f1, filt_w2, filt_b2, filt_f2, filt_w3, filt_b3, filt_f3, filt_w4, hy_skip, w_hy_proj, w_out, ffn_norm_g, w_gate, w_up, w_down, final_norm_g):
    b, seq, d = x.shape
    depth = w_in.shape[0]
    qtab, kcos, ksin = _rope_tables(positions)
    vone, place = _static_tables()
    table = _dft_table()
    table_b = table.astype(BF16)
    fw = dict(w1=filt_w1, b1=filt_b1, f1=filt_f1, w2=filt_w2, b2=filt_b2, f2=filt_f2,
              w3=filt_w3, b3=filt_b3, f3=filt_f3, w4=filt_w4)
    g_cos, g_sin = _filter_spectrum(seq, table, fw)

    x2 = x.reshape(b * seq, d)
    for i in range(depth):
        w1, why, wga, wgb, wqt, wk, wvt = _layer_weights(w_in[i], w_q_up[i], w_kv_up[i])
        q3, k2, vt, hy2 = _proj_call(
            x2, seq, mix_norm_g[i][None], w1, why, q_norm_g[i][None], wqt, kv_norm_g[i][None],
            wk, wvt, vone, place, qtab, kcos, ksin)
        ao = _attn_call(q3, k2.reshape(b, seq, QKV_W), vt)
        hy = _hyena_call(
            hy2.reshape(b, seq, 3 * D_HY), hy_conv_w[i].reshape(9, D_HY), hy_conv_b[i].reshape(3, D_HY),
            hy_skip[i][None], g_cos, g_sin, i, table_b, table_b.T)
        x2 = _merge_call(
            x2, ao.reshape(b * seq, -1), hy.reshape(b * seq, -1), mix_norm_g[i][None], wga, wgb,
            w_attn_proj[i].astype(BF16), w_hy_proj[i].astype(BF16), w_out[i].astype(BF16),
            ffn_norm_g[i][None], w_gate[i].astype(BF16), w_up[i].astype(BF16), w_down[i].astype(BF16),
            final_norm_g[None], final_norm=(i == depth - 1))
    return x2.reshape(b, seq, d)
```
